```python
import jax, jax.numpy as jnp
from jax import lax
import numpy as np

D_MODEL = 1024
BATCH = 1
SEQ = 16384
DEPTH = 1
DEC_BATCH = 8
DEC_SEQ = 2048
PAST_LEN = 128

MIX_WIDTH = D_MODEL
GLA_WIDTH = MIX_WIDTH // 2
CONV_WIDTH = MIX_WIDTH - GLA_WIDTH
GLA_HEADS = 4
GLA_DV = GLA_WIDTH // GLA_HEADS
GLA_KEY_WIDTH = GLA_WIDTH // 2
GLA_DK = GLA_KEY_WIDTH // GLA_HEADS
GATE_RANK = 16
GATE_TAU = 16.0
CHUNK = 64
CONV_GROUPS = 8
CONV_GROUP_DIM = CONV_WIDTH // CONV_GROUPS
CONV_K = 3
D_FF = 2816
FFN_RESIDUAL = 0.5
EPS = 1e-6
IN_SPLIT_SIZES = (GLA_KEY_WIDTH, GLA_KEY_WIDTH, GLA_WIDTH, GLA_WIDTH,
                  GATE_RANK, GATE_RANK, CONV_WIDTH, CONV_WIDTH, CONV_WIDTH)
N_IN = 2 * GLA_KEY_WIDTH + 2 * GLA_WIDTH + 2 * GATE_RANK + 3 * CONV_WIDTH

kernel_name = "hybrid_gla_shortconv_macaron_encoder"


def rms_norm(x, gain):
    xf = x.astype(jnp.float32)
    inv = lax.rsqrt(jnp.mean(xf * xf, axis=-1, keepdims=True) + EPS)
    return (xf * inv).astype(x.dtype) * gain


def swiglu(x, w_gate, w_up, w_down):
    return (jax.nn.silu(x @ w_gate) * (x @ w_up)) @ w_down


def gla_chunked(q, k, v, log_a):
    bsz, nh, seq_len, dk = q.shape
    dv = v.shape[-1]
    n_chunks = seq_len // CHUNK

    def to_chunks(t):
        return t.reshape(bsz, nh, n_chunks, CHUNK, t.shape[-1]).transpose(2, 0, 1, 3, 4)

    qc, kc, vc, ac = (to_chunks(t) for t in (q, k, v, log_a))
    cum = jnp.cumsum(ac, axis=3)
    total = cum[:, :, :, -1:, :]
    q_dec = qc * jnp.exp(cum)
    k_inv = kc * jnp.exp(-cum)
    k_tail = kc * jnp.exp(total - cum)
    causal_in_chunk = jnp.tril(jnp.ones((CHUNK, CHUNK), dtype=bool))
    scores = jnp.einsum('nbhid,nbhjd->nbhij', q_dec, k_inv)
    intra = jnp.einsum('nbhij,nbhjv->nbhiv', jnp.where(causal_in_chunk, scores, 0.0), vc)

    def step(state, inp):
        qd, kt, vv, tot = inp
        out = jnp.einsum('bhid,bhdv->bhiv', qd, state)
        state = state * jnp.exp(tot[:, :, 0, :])[..., None] + jnp.einsum('bhjd,bhjv->bhdv', kt, vv)
        return state, out

    s0 = jnp.zeros((bsz, nh, dk, dv), jnp.float32)
    _, inter = lax.scan(step, s0, (q_dec, k_tail, vc, total))
    o = intra + inter
    return o.transpose(1, 2, 0, 3, 4).reshape(bsz, nh, seq_len, dv)


def bidirectional_gla(q, k, v, log_a_fwd, log_a_bwd):
    flip = lambda t: jnp.flip(t, axis=2)
    fwd = gla_chunked(q, k, v, log_a_fwd)
    bwd = flip(gla_chunked(flip(q), flip(k), flip(v), flip(log_a_bwd)))
    return fwd + bwd


def token_mixer(u, w_in, gate_fwd_w, gate_fwd_b, gate_bwd_w, gate_bwd_b,
                gla_head_norm, conv_w, conv_group_norm, w_out):
    bsz, seq_len, _ = u.shape
    split_points = np.cumsum(IN_SPLIT_SIZES)[:-1].tolist()
    q, k, v, g, r_f, r_b, b_gate, c_gate, h_in = jnp.split(u @ w_in, split_points, axis=-1)

    def heads(t, d):
        return t.reshape(bsz, seq_len, GLA_HEADS, d).transpose(0, 2, 1, 3).astype(jnp.float32)

    qh = heads(q, GLA_DK) * (GLA_DK ** -0.5)
    kh = heads(k, GLA_DK)
    vh = heads(v, GLA_DV)
    log_a_f = heads(jax.nn.log_sigmoid((r_f @ gate_fwd_w + gate_fwd_b).astype(jnp.float32)), GLA_DK) / GATE_TAU
    log_a_b = heads(jax.nn.log_sigmoid((r_b @ gate_bwd_w + gate_bwd_b).astype(jnp.float32)), GLA_DK) / GATE_TAU
    o = bidirectional_gla(qh, kh, vh, log_a_f, log_a_b)
    o = o * lax.rsqrt(jnp.mean(o * o, axis=-1, keepdims=True) + EPS)
    o = o.transpose(0, 2, 1, 3).reshape(bsz, seq_len, GLA_WIDTH).astype(u.dtype)
    gla_out = o * gla_head_norm * jax.nn.silu(g)

    z = c_gate * h_in
    pad = CONV_K // 2
    zp = jnp.pad(z, ((0, 0), (pad, pad), (0, 0)))
    conv = sum(zp[:, i:i + seq_len] * conv_w[i] for i in range(CONV_K))
    yc = (b_gate * conv).reshape(bsz, seq_len, CONV_GROUPS, CONV_GROUP_DIM).astype(jnp.float32)
    yc = yc * lax.rsqrt(jnp.mean(yc * yc, axis=-1, keepdims=True) + EPS)
    conv_out = yc.reshape(bsz, seq_len, CONV_WIDTH).astype(u.dtype) * conv_group_norm

    return jnp.concatenate([gla_out, conv_out], axis=-1) @ w_out


def encoder_layer(x, ffn1_norm, ffn1_w_gate, ffn1_w_up, ffn1_w_down, mix_norm, w_in,
                  gate_fwd_w, gate_fwd_b, gate_bwd_w, gate_bwd_b, gla_head_norm,
                  conv_w, conv_group_norm, w_out, ffn2_norm, ffn2_w_gate, ffn2_w_up, ffn2_w_down):
    x = x + FFN_RESIDUAL * swiglu(rms_norm(x, ffn1_norm), ffn1_w_gate, ffn1_w_up, ffn1_w_down)
    x = x + token_mixer(rms_norm(x, mix_norm), w_in, gate_fwd_w, gate_fwd_b, gate_bwd_w, gate_bwd_b,
                        gla_head_norm, conv_w, conv_group_norm, w_out)
    x = x + FFN_RESIDUAL * swiglu(rms_norm(x, ffn2_norm), ffn2_w_gate, ffn2_w_up, ffn2_w_down)
    return x


def trunk(x, layer_params, final_norm):
    for l in range(DEPTH):
        x = encoder_layer(x, *[p[l] for p in layer_params])
    return rms_norm(x, final_norm)


def setup_inputs(seed: int = 0) -> dict:
    key = jax.random.key(seed)
    ks = jax.random.split(key, 24)
    f32 = jnp.float32
    nrm = lambda k, shape, scale: jax.random.normal(k, shape, f32) * scale
    gain = lambda k, shape: 1.0 + 0.02 * jax.random.normal(k, shape, f32)
    return {
        "x_prompt": jax.random.normal(ks[0], (BATCH, SEQ, D_MODEL), f32),
        "x_sample": jax.random.normal(ks[1], (DEC_BATCH, DEC_SEQ, D_MODEL), f32),
        "ffn1_norm": gain(ks[2], (DEPTH, D_MODEL)),
        "ffn1_w_gate": nrm(ks[3], (DEPTH, D_MODEL, D_FF), D_MODEL ** -0.5),
        "ffn1_w_up": nrm(ks[4], (DEPTH, D_MODEL, D_FF), D_MODEL ** -0.5),
        "ffn1_w_down": nrm(ks[5], (DEPTH, D_FF, D_MODEL), D_FF ** -0.5),
        "mix_norm": gain(ks[6], (DEPTH, D_MODEL)),
        "w_in": nrm(ks[7], (DEPTH, D_MODEL, N_IN), D_MODEL ** -0.5),
        "gate_fwd_w": nrm(ks[8], (DEPTH, GATE_RANK, GLA_KEY_WIDTH), GATE_RANK ** -0.5),
        "gate_fwd_b": nrm(ks[9], (DEPTH, GLA_KEY_WIDTH), 0.1),
        "gate_bwd_w": nrm(ks[10], (DEPTH, GATE_RANK, GLA_KEY_WIDTH), GATE_RANK ** -0.5),
        "gate_bwd_b": nrm(ks[11], (DEPTH, GLA_KEY_WIDTH), 0.1),
        "gla_head_norm": gain(ks[12], (DEPTH, GLA_WIDTH)),
        "conv_w": nrm(ks[13], (DEPTH, CONV_K, CONV_WIDTH), CONV_K ** -0.5),
        "conv_group_norm": gain(ks[14], (DEPTH, CONV_WIDTH)),
        "w_out": nrm(ks[15], (DEPTH, MIX_WIDTH, D_MODEL), MIX_WIDTH ** -0.5),
        "ffn2_norm": gain(ks[16], (DEPTH, D_MODEL)),
        "ffn2_w_gate": nrm(ks[17], (DEPTH, D_MODEL, D_FF), D_MODEL ** -0.5),
        "ffn2_w_up": nrm(ks[18], (DEPTH, D_MODEL, D_FF), D_MODEL ** -0.5),
        "ffn2_w_down": nrm(ks[19], (DEPTH, D_FF, D_MODEL), D_FF ** -0.5),
        "final_norm": gain(ks[20], (D_MODEL,)),
    }


def reference(x_prompt, x_sample, ffn1_norm, ffn1_w_gate, ffn1_w_up, ffn1_w_down, mix_norm,
              w_in, gate_fwd_w, gate_fwd_b, gate_bwd_w, gate_bwd_b, gla_head_norm, conv_w,
              conv_group_norm, w_out, ffn2_norm, ffn2_w_gate, ffn2_w_up, ffn2_w_down, final_norm):
    layer_params = (ffn1_norm, ffn1_w_gate, ffn1_w_up, ffn1_w_down, mix_norm, w_in,
                    gate_fwd_w, gate_fwd_b, gate_bwd_w, gate_bwd_b, gla_head_norm,
                    conv_w, conv_group_norm, w_out, ffn2_norm, ffn2_w_gate, ffn2_w_up, ffn2_w_down)
    y_prompt = trunk(x_prompt, layer_params, final_norm)
    y_sample = trunk(x_sample, layer_params, final_norm)
    return (y_prompt, y_sample)
```

```python
import functools

import jax
import jax.numpy as jnp
from jax import lax
from jax.experimental import pallas as pl
from jax.experimental.pallas import tpu as pltpu

F32 = jnp.float32
BF16 = jnp.bfloat16

D_MODEL = 1024
D_FF = 2816
GLA_WIDTH = 512
CONV_WIDTH = 512
GLA_HEADS = 4
GLA_DV = 128
GLA_KEY_WIDTH = 256
GLA_DK = 64
GATE_RANK = 16
GATE_TAU = 16.0
CHUNK = 64
CONV_GROUP_DIM = 64
FFN_RESIDUAL = 0.5
EPS = 1e-6

TOKEN_TILE = 512
FF_CHUNK = 256
GLA_TILE = 256
VMEM_LIMIT_BYTES = 56 * 1024 * 1024

_NT = (((1,), (1,)), ((), ()))
_TN = (((0,), (0,)), ((), ()))


def _dot(a, b):
    return jnp.dot(a, b, preferred_element_type=F32)


def _rms(x, gain):
    inv = lax.rsqrt(jnp.mean(x * x, axis=-1, keepdims=True) + EPS)
    return (x * inv) * gain


def _swiglu(xn, wg_ref, wu_ref, wd_ref, h_ref):
    for c in range(D_FF // FF_CHUNK):
        cols = slice(c * FF_CHUNK, (c + 1) * FF_CHUNK)
        g = _dot(xn, wg_ref[:, cols])
        u = _dot(xn, wu_ref[:, cols])
        h_ref[:, cols] = ((g * jax.nn.sigmoid(g)) * u).astype(BF16)
    return _dot(h_ref[...], wd_ref[...])


def _kernel_a(x_ref, n1_ref, wg_ref, wu_ref, wd_ref, mn_ref, wqkvg_ref, wr_ref, wbch_ref,
              wgate_ref, bgate_ref,
              x1_ref, q_ref, k_ref, v_ref, sg_ref, la_ref, b_ref, z_ref, h_ref):
    x = x_ref[...]
    xn = _rms(x, n1_ref[...]).astype(BF16)
    x1 = x + FFN_RESIDUAL * _swiglu(xn, wg_ref, wu_ref, wd_ref, h_ref)
    x1_ref[...] = x1
    u = _rms(x1, mn_ref[...]).astype(BF16)

    p = _dot(u, wqkvg_ref[...])
    q_ref[...] = p[:, :GLA_KEY_WIDTH] * (GLA_DK ** -0.5)
    k_ref[...] = p[:, GLA_KEY_WIDTH:2 * GLA_KEY_WIDTH]
    v_ref[...] = p[:, 2 * GLA_KEY_WIDTH:2 * GLA_KEY_WIDTH + GLA_WIDTH]
    g = p[:, 2 * GLA_KEY_WIDTH + GLA_WIDTH:]
    sg_ref[...] = g * jax.nn.sigmoid(g)

    r = _dot(u, wr_ref[...])
    t = _dot(r.astype(BF16), wgate_ref[...]) + bgate_ref[...]
    log_sig = jnp.minimum(t, 0.0) - jnp.log1p(jnp.exp(-jnp.abs(t)))
    la_ref[...] = log_sig * (1.0 / GATE_TAU)

    pb = _dot(u, wbch_ref[...])
    b_ref[...] = pb[:, :CONV_WIDTH]
    z_ref[...] = pb[:, CONV_WIDTH:2 * CONV_WIDTH] * pb[:, 2 * CONV_WIDTH:]


def _resident(shape):
    return pl.BlockSpec(shape, lambda *_: (0,) * len(shape), pipeline_mode=pl.Buffered(1))


def _call_a(x, n1, wg, wu, wd, mn, wqkvg, wr, wbch, wgate, bgate):
    t = x.shape[0]
    tm = TOKEN_TILE
    row = lambda w: pl.BlockSpec((tm, w), lambda i: (i, 0))
    out_w = (D_MODEL, GLA_KEY_WIDTH, GLA_KEY_WIDTH, GLA_WIDTH, GLA_WIDTH, 2 * GLA_KEY_WIDTH,
             CONV_WIDTH, CONV_WIDTH)
    weights = (n1, wg, wu, wd, mn, wqkvg, wr, wbch, wgate, bgate)
    return pl.pallas_call(
        _kernel_a,
        grid=(t // tm,),
        in_specs=[row(D_MODEL)] + [_resident(w.shape) for w in weights],
        out_specs=[row(w) for w in out_w],
        out_shape=[jax.ShapeDtypeStruct((t, w), F32) for w in out_w],
        scratch_shapes=[pltpu.VMEM((tm, D_FF), BF16)],
        compiler_params=pltpu.CompilerParams(
            dimension_semantics=("arbitrary",), vmem_limit_bytes=VMEM_LIMIT_BYTES),
        name="ffn1_inproj",
    )(x, *weights)


def _gla_tile(q_ref, k_ref, v_ref, la_ref, o_ref, st_ref, reverse):
    rows = GLA_TILE
    n_chunks = rows // CHUNK
    ri = lax.broadcasted_iota(jnp.int32, (rows, rows), 0)
    ci = lax.broadcasted_iota(jnp.int32, (rows, rows), 1)
    same_chunk = (ri // CHUNK) == (ci // CHUNK)
    tri = jnp.where(same_chunk & ((ci >= ri) if reverse else (ci <= ri)), 1.0, 0.0).astype(BF16)

    la = la_ref[...]
    la_hi = la.astype(BF16)
    la_lo = (la - la_hi.astype(F32)).astype(BF16)
    cum = _dot(tri, la_hi) + _dot(tri, la_lo)

    si = lax.broadcasted_iota(jnp.int32, (GLA_HEADS * CHUNK, CHUNK), 0) % CHUNK
    sj = lax.broadcasted_iota(jnp.int32, (GLA_HEADS * CHUNK, CHUNK), 1)
    keep = (sj >= si) if reverse else (sj <= si)
    lane = lax.broadcasted_iota(jnp.int32, (1, GLA_KEY_WIDTH), 1)
    head_masks = [jnp.where((lane // GLA_DK) == h, 1.0, 0.0) for h in range(GLA_HEADS)]
    bd_r = lax.broadcasted_iota(jnp.int32, (GLA_WIDTH, GLA_KEY_WIDTH), 0) // GLA_DV
    bd_c = lax.broadcasted_iota(jnp.int32, (GLA_WIDTH, GLA_KEY_WIDTH), 1) // GLA_DK
    block_diag = jnp.where(bd_r == bd_c, 1.0, 0.0)

    order = range(n_chunks - 1, -1, -1) if reverse else range(n_chunks)
    for c in order:
        rs = slice(c * CHUNK, (c + 1) * CHUNK)
        cum_c = cum[rs]
        total = cum_c[0:1] if reverse else cum_c[CHUNK - 1:CHUNK]
        qc = q_ref[rs, :]
        kc = k_ref[rs, :]
        vc = v_ref[rs, :].astype(BF16)
        qd = qc * jnp.exp(cum_c)
        k_inv = (kc * jnp.exp(-cum_c)).astype(BF16)
        k_tail = (kc * jnp.exp(total - cum_c)).astype(BF16)
        q_heads = jnp.concatenate([qd * m for m in head_masks], axis=0).astype(BF16)
        s = lax.dot_general(q_heads, k_inv, _NT, preferred_element_type=F32)
        s = jnp.where(keep, s, 0.0).astype(BF16)
        intra = jnp.concatenate(
            [_dot(s[h * CHUNK:(h + 1) * CHUNK], vc[:, h * GLA_DV:(h + 1) * GLA_DV])
             for h in range(GLA_HEADS)], axis=1)
        st = st_ref[...]
        inter = lax.dot_general(qd.astype(BF16), st.astype(BF16), _NT, preferred_element_type=F32)
        o_ref[rs, :] = intra + inter
        upd = lax.dot_general(vc, k_tail, _TN, preferred_element_type=F32)
        st_ref[...] = st * jnp.exp(total) + upd * block_diag


def _kernel_b(qf_ref, kf_ref, vf_ref, laf_ref, qb_ref, kb_ref, vb_ref, lab_ref,
              of_ref, ob_ref, stf_ref, stb_ref):
    @pl.when(pl.program_id(1) == 0)
    def _():
        stf_ref[...] = jnp.zeros_like(stf_ref)
        stb_ref[...] = jnp.zeros_like(stb_ref)

    _gla_tile(qf_ref, kf_ref, vf_ref, laf_ref, of_ref, stf_ref, reverse=False)
    _gla_tile(qb_ref, kb_ref, vb_ref, lab_ref, ob_ref, stb_ref, reverse=True)


def _call_b(q, k, v, la, seq_len):
    t = q.shape[0]
    n_seq = t // seq_len
    nt = seq_len // GLA_TILE
    fwd = lambda s, j: (s * nt + j, 0)
    bwd = lambda s, j: (s * nt + (nt - 1 - j), 0)
    bwd_la = lambda s, j: (s * nt + (nt - 1 - j), 1)
    kw = GLA_KEY_WIDTH
    spec = lambda w, im: pl.BlockSpec((GLA_TILE, w), im)
    return pl.pallas_call(
        _kernel_b,
        grid=(n_seq, nt),
        in_specs=[spec(kw, fwd), spec(kw, fwd), spec(GLA_WIDTH, fwd), spec(kw, fwd),
                  spec(kw, bwd), spec(kw, bwd), spec(GLA_WIDTH, bwd), spec(kw, bwd_la)],
        out_specs=[spec(GLA_WIDTH, fwd), spec(GLA_WIDTH, bwd)],
        out_shape=[jax.ShapeDtypeStruct((t, GLA_WIDTH), F32)] * 2,
        scratch_shapes=[pltpu.VMEM((GLA_WIDTH, GLA_KEY_WIDTH), F32)] * 2,
        compiler_params=pltpu.CompilerParams(dimension_semantics=("arbitrary", "arbitrary")),
        name="gla_bidir",
    )(q, k, v, la, q, k, v, la)


def _kernel_c(x1_ref, of_ref, ob_ref, sg_ref, b_ref, z_ref, zp_ref, zn_ref,
              hn_ref, cw_ref, cn_ref, gm_ref, wo_ref, n2_ref, wg_ref, wu_ref, wd_ref, fn_ref,
              y_ref, h_ref, *, tiles_per_seq):
    i = pl.program_id(0)
    tm = TOKEN_TILE

    o = of_ref[...] + ob_ref[...]
    heads = []
    for h in range(GLA_HEADS):
        oh = o[:, h * GLA_DV:(h + 1) * GLA_DV]
        heads.append(oh * lax.rsqrt(jnp.mean(oh * oh, axis=-1, keepdims=True) + EPS))
    gla_out = (jnp.concatenate(heads, axis=1) * hn_ref[...]) * sg_ref[...]

    z = z_ref[...]
    row = lax.broadcasted_iota(jnp.int32, (tm, 1), 0)
    first_of_seq = (i % tiles_per_seq) == 0
    last_of_seq = (i % tiles_per_seq) == (tiles_per_seq - 1)
    prev_row = jnp.where(first_of_seq, 0.0, zp_ref[7:8, :])
    next_row = jnp.where(last_of_seq, 0.0, zn_ref[0:1, :])
    z_prev = jnp.where(row == 0, prev_row, pltpu.roll(z, 1, axis=0))
    z_next = jnp.where(row == tm - 1, next_row, pltpu.roll(z, tm - 1, axis=0))
    conv = (z_prev * cw_ref[0:1, :] + z * cw_ref[1:2, :]) + z_next * cw_ref[2:3, :]
    yc = b_ref[...] * conv
    sq = yc * yc
    sq_hi = sq.astype(BF16)
    sq_lo = (sq - sq_hi.astype(F32)).astype(BF16)
    gm = gm_ref[...]
    ms = _dot(sq_hi, gm) + _dot(sq_lo, gm)
    conv_out = (yc * lax.rsqrt(ms + EPS)) * cn_ref[...]

    mix = jnp.concatenate([gla_out, conv_out], axis=1).astype(BF16)
    x2 = x1_ref[...] + _dot(mix, wo_ref[...])
    xn = _rms(x2, n2_ref[...]).astype(BF16)
    x3 = x2 + FFN_RESIDUAL * _swiglu(xn, wg_ref, wu_ref, wd_ref, h_ref)
    y_ref[...] = _rms(x3, fn_ref[...])


def _call_c(x1, o_f, o_b, sg, b, z, hn, cw, cn, gm, wo, n2, wg, wu, wd, fn, seq_len):
    t = x1.shape[0]
    tm = TOKEN_TILE
    n8 = tm // 8
    row = lambda w: pl.BlockSpec((tm, w), lambda i: (i, 0))
    prev8 = pl.BlockSpec((8, CONV_WIDTH), lambda i: (jnp.maximum(i * n8 - 1, 0), 0))
    next8 = pl.BlockSpec((8, CONV_WIDTH), lambda i: (jnp.minimum((i + 1) * n8, t // 8 - 1), 0))
    weights = (hn, cw, cn, gm, wo, n2, wg, wu, wd, fn)
    return pl.pallas_call(
        functools.partial(_kernel_c, tiles_per_seq=seq_len // tm),
        grid=(t // tm,),
        in_specs=[row(D_MODEL), row(GLA_WIDTH), row(GLA_WIDTH), row(GLA_WIDTH), row(CONV_WIDTH),
                  row(CONV_WIDTH), prev8, next8] + [_resident(w.shape) for w in weights],
        out_specs=row(D_MODEL),
        out_shape=jax.ShapeDtypeStruct((t, D_MODEL), F32),
        scratch_shapes=[pltpu.VMEM((tm, D_FF), BF16)],
        compiler_params=pltpu.CompilerParams(
            dimension_semantics=("arbitrary",), vmem_limit_bytes=VMEM_LIMIT_BYTES),
        name="mix_out_ffn2",
    )(x1, o_f, o_b, sg, b, z, z, z, *weights)


def _trunk(x, seq_len, p):
    shape = x.shape
    x = x.reshape(-1, D_MODEL)
    x1, q, k, v, sg, la, b, z = _call_a(
        x, p["n1"], p["wg1"], p["wu1"], p["wd1"], p["mn"], p["wqkvg"], p["wr"], p["wbch"],
        p["wgate"], p["bgate"])
    o_f, o_b = _call_b(q, k, v, la, seq_len)
    y = _call_c(x1, o_f, o_b, sg, b, z, p["hn"], p["cw"], p["cn"], p["gm"], p["wo"], p["n2"],
                p["wg2"], p["wu2"], p["wd2"], p["fn"], seq_len)
    return y.reshape(shape)


def kernel(x_prompt, x_sample, ffn1_norm, ffn1_w_gate, ffn1_w_up, ffn1_w_down, mix_norm, w_in,
           gate_fwd_w, gate_fwd_b, gate_bwd_w, gate_bwd_b, gla_head_norm, conv_w, conv_group_norm,
           w_out, ffn2_norm, ffn2_w_gate, ffn2_w_up, ffn2_w_down, final_norm):
    depth = ffn1_norm.shape[0]
    o_r = 2 * GLA_KEY_WIDTH + 2 * GLA_WIDTH
    o_b = o_r + 2 * GATE_RANK
    zeros = jnp.zeros((GATE_RANK, GLA_KEY_WIDTH), F32)
    grp = jnp.arange(CONV_WIDTH) // CONV_GROUP_DIM
    group_mean = jnp.where(grp[:, None] == grp[None, :], 1.0 / CONV_GROUP_DIM, 0.0).astype(BF16)

    xs = (x_prompt, x_sample)
    for l in range(depth):
        row = lambda a: a[l].reshape(1, -1)
        p = dict(
            n1=row(ffn1_norm), wg1=ffn1_w_gate[l].astype(BF16), wu1=ffn1_w_up[l].astype(BF16),
            wd1=ffn1_w_down[l].astype(BF16), mn=row(mix_norm),
            wqkvg=w_in[l, :, :o_r].astype(BF16), wr=w_in[l, :, o_r:o_b].astype(BF16),
            wbch=w_in[l, :, o_b:].astype(BF16),
            wgate=jnp.concatenate(
                [jnp.concatenate([gate_fwd_w[l], zeros], axis=1),
                 jnp.concatenate([zeros, gate_bwd_w[l]], axis=1)], axis=0).astype(BF16),
            bgate=jnp.concatenate([gate_fwd_b[l], gate_bwd_b[l]]).reshape(1, -1),
            hn=row(gla_head_norm), cw=conv_w[l], cn=row(conv_group_norm), gm=group_mean,
            wo=w_out[l].astype(BF16), n2=row(ffn2_norm), wg2=ffn2_w_gate[l].astype(BF16),
            wu2=ffn2_w_up[l].astype(BF16), wd2=ffn2_w_down[l].astype(BF16),
            fn=final_norm.reshape(1, -1))
        assert depth == 1
        xs = tuple(_trunk(x, x.shape[1], p) for x in xs)
    return xs
```

```python
import functools

import jax
import jax.numpy as jnp
from jax import lax
from jax.experimental import pallas as pl
from jax.experimental.pallas import tpu as pltpu

F32 = jnp.float32
BF16 = jnp.bfloat16

D_MODEL = 1024
D_FF = 2816
GLA_WIDTH = 512
CONV_WIDTH = 512
GLA_HEADS = 4
GLA_DV = 128
GLA_KEY_WIDTH = 256
GLA_DK = 64
GATE_RANK = 16
GATE_TAU = 16.0
CHUNK = 64
CONV_GROUP_DIM = 64
FFN_RESIDUAL = 0.5
EPS = 1e-6
LANES = 128

TOKEN_TILE = 512
FF_CHUNK = 256
GLA_TILE = 256
VMEM_LIMIT_BYTES = 56 * 1024 * 1024

_NT = (((1,), (1,)), ((), ()))
_TN = (((0,), (0,)), ((), ()))


def _dot(a, b):
    return jnp.dot(a, b, preferred_element_type=F32)


def _rms(x, gain):
    inv = lax.rsqrt(jnp.mean(x * x, axis=-1, keepdims=True) + EPS)
    return (x * inv) * gain


def _swiglu(xn, wg_ref, wu_ref, wd_ref, h_ref):
    for c in range(D_FF // FF_CHUNK):
        cols = slice(c * FF_CHUNK, (c + 1) * FF_CHUNK)
        g = _dot(xn, wg_ref[:, cols])
        u = _dot(xn, wu_ref[:, cols])
        h_ref[:, cols] = ((g * jax.nn.sigmoid(g)) * u).astype(BF16)
    return _dot(h_ref[...], wd_ref[...])


def _kernel_a(x_ref, n1_ref, wg_ref, wu_ref, wd_ref, mn_ref, wqkvg_ref, wr_ref, wbch_ref,
              wgate_ref, bgate_ref,
              x1_ref, q_ref, k_ref, v_ref, sg_ref, la_ref, b_ref, z_ref, h_ref):
    x = x_ref[...]
    xn = _rms(x, n1_ref[...]).astype(BF16)
    x1 = x + FFN_RESIDUAL * _swiglu(xn, wg_ref, wu_ref, wd_ref, h_ref)
    x1_ref[...] = x1
    u = _rms(x1, mn_ref[...]).astype(BF16)

    p = _dot(u, wqkvg_ref[...])
    q_ref[...] = p[:, :GLA_KEY_WIDTH] * (GLA_DK ** -0.5)
    k_ref[...] = p[:, GLA_KEY_WIDTH:2 * GLA_KEY_WIDTH]
    v_ref[...] = p[:, 2 * GLA_KEY_WIDTH:2 * GLA_KEY_WIDTH + GLA_WIDTH]
    g = p[:, 2 * GLA_KEY_WIDTH + GLA_WIDTH:]
    sg_ref[...] = g * jax.nn.sigmoid(g)

    r = _dot(u, wr_ref[...])
    t = _dot(r.astype(BF16), wgate_ref[...]) + bgate_ref[...]
    log_sig = jnp.minimum(t, 0.0) - jnp.log1p(jnp.exp(-jnp.abs(t)))
    la_ref[...] = log_sig * (1.0 / GATE_TAU)

    pb = _dot(u, wbch_ref[...])
    b_ref[...] = pb[:, :CONV_WIDTH]
    z_ref[...] = pb[:, CONV_WIDTH:2 * CONV_WIDTH] * pb[:, 2 * CONV_WIDTH:]


def _resident(shape):
    return pl.BlockSpec(shape, lambda *_: (0,) * len(shape), pipeline_mode=pl.Buffered(1))


def _call_a(x, n1, wg, wu, wd, mn, wqkvg, wr, wbch, wgate, bgate):
    t = x.shape[0]
    tm = TOKEN_TILE
    row = lambda w: pl.BlockSpec((tm, w), lambda i: (i, 0))
    out_w = (D_MODEL, GLA_KEY_WIDTH, GLA_KEY_WIDTH, GLA_WIDTH, GLA_WIDTH, 2 * GLA_KEY_WIDTH,
             CONV_WIDTH, CONV_WIDTH)
    weights = (n1, wg, wu, wd, mn, wqkvg, wr, wbch, wgate, bgate)
    return pl.pallas_call(
        _kernel_a,
        grid=(t // tm,),
        in_specs=[row(D_MODEL)] + [_resident(w.shape) for w in weights],
        out_specs=[row(w) for w in out_w],
        out_shape=[jax.ShapeDtypeStruct((t, w), F32) for w in out_w],
        scratch_shapes=[pltpu.VMEM((tm, D_FF), BF16)],
        compiler_params=pltpu.CompilerParams(
            dimension_semantics=("arbitrary",), vmem_limit_bytes=VMEM_LIMIT_BYTES),
        name="ffn1_inproj",
    )(x, *weights)


def _gla_tile(q_ref, k_ref, v_ref, la_ref, o_ref, st_ref, reverse):
    rows = GLA_TILE
    n_chunks = rows // CHUNK
    head_block = [(h * GLA_DK) // LANES for h in range(GLA_HEADS)]
    head_half = [(h * GLA_DK) % LANES // GLA_DK for h in range(GLA_HEADS)]
    ri = lax.broadcasted_iota(jnp.int32, (rows, rows), 0)
    ci = lax.broadcasted_iota(jnp.int32, (rows, rows), 1)
    keep = ((ri // CHUNK) == (ci // CHUNK)) & ((ci >= ri) if reverse else (ci <= ri))
    tri = jnp.where(keep, 1.0, 0.0).astype(BF16)

    la = la_ref[...]
    la_hi = la.astype(BF16)
    la_lo = (la - la_hi.astype(F32)).astype(BF16)
    cum = _dot(tri, la_hi) + _dot(tri, la_lo)
    totals = [cum[c * CHUNK:c * CHUNK + 1] if reverse else cum[(c + 1) * CHUNK - 1:(c + 1) * CHUNK]
              for c in range(n_chunks)]
    total_rows = jnp.concatenate(
        [jnp.broadcast_to(t, (CHUNK, GLA_KEY_WIDTH)) for t in totals], axis=0)

    q = q_ref[...]
    k = k_ref[...]
    v = v_ref[...].astype(BF16)
    qd = q * jnp.exp(cum)
    qd_bf = qd.astype(BF16)
    k_inv = (k * jnp.exp(-cum)).astype(BF16)
    k_tail = (k * jnp.exp(total_rows - cum)).astype(BF16)

    lane = lax.broadcasted_iota(jnp.int32, (1, GLA_KEY_WIDTH), 1)
    intra = []
    for h in range(GLA_HEADS):
        qh = jnp.where((lane // GLA_DK) == h, qd, 0.0).astype(BF16)
        s = lax.dot_general(qh, k_inv, _NT, preferred_element_type=F32)
        s = jnp.where(keep, s, 0.0).astype(BF16)
        intra.append(_dot(s, v[:, h * GLA_DV:(h + 1) * GLA_DV]))

    half = lax.broadcasted_iota(jnp.int32, (1, LANES), 1) // GLA_DK
    incr = []
    for c in range(n_chunks):
        rs = slice(c * CHUNK, (c + 1) * CHUNK)
        full = lax.dot_general(v[rs], k_tail[rs], _TN, preferred_element_type=F32)
        incr.append([
            jnp.where(half == head_half[h],
                      full[h * GLA_DV:(h + 1) * GLA_DV, head_block[h] * LANES:(head_block[h] + 1) * LANES],
                      0.0)
            for h in range(GLA_HEADS)])

    st = [st_ref[h * GLA_DV:(h + 1) * GLA_DV, :] for h in range(GLA_HEADS)]
    zeros = jnp.zeros((GLA_DV, LANES), BF16)
    order = range(n_chunks - 1, -1, -1) if reverse else range(n_chunks)
    for c in order:
        rs = slice(c * CHUNK, (c + 1) * CHUNK)
        st_full = jnp.concatenate(
            [jnp.concatenate([st[h].astype(BF16) if b == head_block[h] else zeros
                              for b in range(GLA_KEY_WIDTH // LANES)], axis=1)
             for h in range(GLA_HEADS)], axis=0)
        inter = lax.dot_general(qd_bf[rs], st_full, _NT, preferred_element_type=F32)
        o_ref[rs, :] = jnp.concatenate([x[rs] for x in intra], axis=1) + inter
        decay = jnp.exp(totals[c])
        st = [st[h] * decay[:, head_block[h] * LANES:(head_block[h] + 1) * LANES] + incr[c][h]
              for h in range(GLA_HEADS)]
    for h in range(GLA_HEADS):
        st_ref[h * GLA_DV:(h + 1) * GLA_DV, :] = st[h]


def _kernel_b(qf_ref, kf_ref, vf_ref, laf_ref, qb_ref, kb_ref, vb_ref, lab_ref,
              of_ref, ob_ref, stf_ref, stb_ref):
    @pl.when(pl.program_id(1) == 0)
    def _():
        stf_ref[...] = jnp.zeros_like(stf_ref)
        stb_ref[...] = jnp.zeros_like(stb_ref)

    _gla_tile(qf_ref, kf_ref, vf_ref, laf_ref, of_ref, stf_ref, reverse=False)
    _gla_tile(qb_ref, kb_ref, vb_ref, lab_ref, ob_ref, stb_ref, reverse=True)


def _call_b(q, k, v, la, seq_len):
    t = q.shape[0]
    n_seq = t // seq_len
    nt = seq_len // GLA_TILE
    fwd = lambda s, j: (s * nt + j, 0)
    bwd = lambda s, j: (s * nt + (nt - 1 - j), 0)
    bwd_la = lambda s, j: (s * nt + (nt - 1 - j), 1)
    kw = GLA_KEY_WIDTH
    spec = lambda w, im: pl.BlockSpec((GLA_TILE, w), im)
    return pl.pallas_call(
        _kernel_b,
        grid=(n_seq, nt),
        in_specs=[spec(kw, fwd), spec(kw, fwd), spec(GLA_WIDTH, fwd), spec(kw, fwd),
                  spec(kw, bwd), spec(kw, bwd), spec(GLA_WIDTH, bwd), spec(kw, bwd_la)],
        out_specs=[spec(GLA_WIDTH, fwd), spec(GLA_WIDTH, bwd)],
        out_shape=[jax.ShapeDtypeStruct((t, GLA_WIDTH), F32)] * 2,
        scratch_shapes=[pltpu.VMEM((GLA_WIDTH, LANES), F32)] * 2,
        compiler_params=pltpu.CompilerParams(dimension_semantics=("arbitrary", "arbitrary")),
        name="gla_bidir",
    )(q, k, v, la, q, k, v, la)


def _kernel_c(x1_ref, of_ref, ob_ref, sg_ref, b_ref, z_ref, zp_ref, zn_ref,
              hn_ref, cw_ref, cn_ref, gm_ref, wo_ref, n2_ref, wg_ref, wu_ref, wd_ref, fn_ref,
              y_ref, h_ref, *, tiles_per_seq):
    i = pl.program_id(0)
    tm = TOKEN_TILE

    o = of_ref[...] + ob_ref[...]
    heads = []
    for h in range(GLA_HEADS):
        oh = o[:, h * GLA_DV:(h + 1) * GLA_DV]
        heads.append(oh * lax.rsqrt(jnp.mean(oh * oh, axis=-1, keepdims=True) + EPS))
    gla_out = (jnp.concatenate(heads, axis=1) * hn_ref[...]) * sg_ref[...]

    z = z_ref[...]
    row = lax.broadcasted_iota(jnp.int32, (tm, 1), 0)
    first_of_seq = (i % tiles_per_seq) == 0
    last_of_seq = (i % tiles_per_seq) == (tiles_per_seq - 1)
    prev_row = jnp.where(first_of_seq, 0.0, zp_ref[7:8, :])
    next_row = jnp.where(last_of_seq, 0.0, zn_ref[0:1, :])
    z_prev = jnp.where(row == 0, prev_row, pltpu.roll(z, 1, axis=0))
    z_next = jnp.where(row == tm - 1, next_row, pltpu.roll(z, tm - 1, axis=0))
    conv = (z_prev * cw_ref[0:1, :] + z * cw_ref[1:2, :]) + z_next * cw_ref[2:3, :]
    yc = b_ref[...] * conv
    sq = yc * yc
    sq_hi = sq.astype(BF16)
    sq_lo = (sq - sq_hi.astype(F32)).astype(BF16)
    gm = gm_ref[...]
    ms = _dot(sq_hi, gm) + _dot(sq_lo, gm)
    conv_out = (yc * lax.rsqrt(ms + EPS)) * cn_ref[...]

    mix = jnp.concatenate([gla_out, conv_out], axis=1).astype(BF16)
    x2 = x1_ref[...] + _dot(mix, wo_ref[...])
    xn = _rms(x2, n2_ref[...]).astype(BF16)
    x3 = x2 + FFN_RESIDUAL * _swiglu(xn, wg_ref, wu_ref, wd_ref, h_ref)
    y_ref[...] = _rms(x3, fn_ref[...])


def _call_c(x1, o_f, o_b, sg, b, z, hn, cw, cn, gm, wo, n2, wg, wu, wd, fn, seq_len):
    t = x1.shape[0]
    tm = TOKEN_TILE
    n8 = tm // 8
    row = lambda w: pl.BlockSpec((tm, w), lambda i: (i, 0))
    prev8 = pl.BlockSpec((8, CONV_WIDTH), lambda i: (jnp.maximum(i * n8 - 1, 0), 0))
    next8 = pl.BlockSpec((8, CONV_WIDTH), lambda i: (jnp.minimum((i + 1) * n8, t // 8 - 1), 0))
    weights = (hn, cw, cn, gm, wo, n2, wg, wu, wd, fn)
    return pl.pallas_call(
        functools.partial(_kernel_c, tiles_per_seq=seq_len // tm),
        grid=(t // tm,),
        in_specs=[row(D_MODEL), row(GLA_WIDTH), row(GLA_WIDTH), row(GLA_WIDTH), row(CONV_WIDTH),
                  row(CONV_WIDTH), prev8, next8] + [_resident(w.shape) for w in weights],
        out_specs=row(D_MODEL),
        out_shape=jax.ShapeDtypeStruct((t, D_MODEL), F32),
        scratch_shapes=[pltpu.VMEM((tm, D_FF), BF16)],
        compiler_params=pltpu.CompilerParams(
            dimension_semantics=("arbitrary",), vmem_limit_bytes=VMEM_LIMIT_BYTES),
        name="mix_out_ffn2",
    )(x1, o_f, o_b, sg, b, z, z, z, *weights)


def _trunk(x, seq_len, p):
    shape = x.shape
    x = x.reshape(-1, D_MODEL)
    x1, q, k, v, sg, la, b, z = _call_a(
        x, p["n1"], p["wg1"], p["wu1"], p["wd1"], p["mn"], p["wqkvg"], p["wr"], p["wbch"],
        p["wgate"], p["bgate"])
    o_f, o_b = _call_b(q, k, v, la, seq_len)
    y = _call_c(x1, o_f, o_b, sg, b, z, p["hn"], p["cw"], p["cn"], p["gm"], p["wo"], p["n2"],
                p["wg2"], p["wu2"], p["wd2"], p["fn"], seq_len)
    return y.reshape(shape)


def kernel(x_prompt, x_sample, ffn1_norm, ffn1_w_gate, ffn1_w_up, ffn1_w_down, mix_norm, w_in,
           gate_fwd_w, gate_fwd_b, gate_bwd_w, gate_bwd_b, gla_head_norm, conv_w, conv_group_norm,
           w_out, ffn2_norm, ffn2_w_gate, ffn2_w_up, ffn2_w_down, final_norm):
    depth = ffn1_norm.shape[0]
    o_r = 2 * GLA_KEY_WIDTH + 2 * GLA_WIDTH
    o_b = o_r + 2 * GATE_RANK
    zeros = jnp.zeros((GATE_RANK, GLA_KEY_WIDTH), F32)
    grp = jnp.arange(CONV_WIDTH) // CONV_GROUP_DIM
    group_mean = jnp.where(grp[:, None] == grp[None, :], 1.0 / CONV_GROUP_DIM, 0.0).astype(BF16)

    xs = (x_prompt, x_sample)
    for l in range(depth):
        row = lambda a: a[l].reshape(1, -1)
        p = dict(
            n1=row(ffn1_norm), wg1=ffn1_w_gate[l].astype(BF16), wu1=ffn1_w_up[l].astype(BF16),
            wd1=ffn1_w_down[l].astype(BF16), mn=row(mix_norm),
            wqkvg=w_in[l, :, :o_r].astype(BF16), wr=w_in[l, :, o_r:o_b].astype(BF16),
            wbch=w_in[l, :, o_b:].astype(BF16),
            wgate=jnp.concatenate(
                [jnp.concatenate([gate_fwd_w[l], zeros], axis=1),
                 jnp.concatenate([zeros, gate_bwd_w[l]], axis=1)], axis=0).astype(BF16),
            bgate=jnp.concatenate([gate_fwd_b[l], gate_bwd_b[l]]).reshape(1, -1),
            hn=row(gla_head_norm), cw=conv_w[l], cn=row(conv_group_norm), gm=group_mean,
            wo=w_out[l].astype(BF16), n2=row(ffn2_norm), wg2=ffn2_w_gate[l].astype(BF16),
            wu2=ffn2_w_up[l].astype(BF16), wd2=ffn2_w_down[l].astype(BF16),
            fn=final_norm.reshape(1, -1))
        assert depth == 1
        xs = tuple(_trunk(x, x.shape[1], p) for x in xs)
    return xs
```

```python
import functools

import jax
import jax.numpy as jnp
from jax import lax
from jax.experimental import pallas as pl
from jax.experimental.pallas import tpu as pltpu

F32 = jnp.float32
BF16 = jnp.bfloat16

D_MODEL = 1024
D_FF = 2816
GLA_WIDTH = 512
CONV_WIDTH = 512
GLA_HEADS = 4
GLA_DV = 128
GLA_KEY_WIDTH = 256
GLA_DK = 64
GATE_RANK = 16
GATE_TAU = 16.0
CHUNK = 64
CONV_GROUP_DIM = 64
FFN_RESIDUAL = 0.5
EPS = 1e-6
LANES = 128
SUBLANES = 8

TOKEN_TILE = 512
FF_CHUNK = 256
GLA_TILE = 256
GLA_STEP_TILES = 4
VMEM_LIMIT_BYTES = 56 * 1024 * 1024

_NT = (((1,), (1,)), ((), ()))
_TN = (((0,), (0,)), ((), ()))


def _dot(a, b):
    return jnp.dot(a, b, preferred_element_type=F32)


def _rms(x, gain):
    inv = lax.rsqrt(jnp.mean(x * x, axis=-1, keepdims=True) + EPS)
    return (x * inv) * gain


def _swiglu(xn, wg_ref, wu_ref, wd_ref, h_ref):
    for c in range(D_FF // FF_CHUNK):
        cols = slice(c * FF_CHUNK, (c + 1) * FF_CHUNK)
        g = _dot(xn, wg_ref[:, cols])
        u = _dot(xn, wu_ref[:, cols])
        h_ref[:, cols] = ((g * jax.nn.sigmoid(g)) * u).astype(BF16)
    return _dot(h_ref[...], wd_ref[...])


def _resident(shape):
    return pl.BlockSpec(shape, lambda *_: (0,) * len(shape), pipeline_mode=pl.Buffered(1))


def _starts_sequence(block, rows, layout):
    prompt_rows, prompt_len, sample_len = layout
    n_prompt = prompt_rows // rows
    in_prompt = block < n_prompt
    return jnp.where(in_prompt, block % (prompt_len // rows) == 0,
                     (block - n_prompt) % (sample_len // rows) == 0)


def _ends_sequence(block, rows, layout):
    prompt_rows, prompt_len, sample_len = layout
    n_prompt = prompt_rows // rows
    in_prompt = block < n_prompt
    return jnp.where(in_prompt, (block + 1) % (prompt_len // rows) == 0,
                     (block - n_prompt + 1) % (sample_len // rows) == 0)


def _kernel_a(xp_ref, xs_ref, n1_ref, wg_ref, wu_ref, wd_ref, mn_ref, wqkvg_ref, wr_ref, wbch_ref,
              wgate_ref, bgate_ref,
              x1_ref, q_ref, k_ref, v_ref, sg_ref, la_ref, b_ref, z_ref, h_ref, *, n_prompt_tiles):
    x = jnp.where(pl.program_id(0) < n_prompt_tiles, xp_ref[...], xs_ref[...])
    xn = _rms(x, n1_ref[...]).astype(BF16)
    x1 = x + FFN_RESIDUAL * _swiglu(xn, wg_ref, wu_ref, wd_ref, h_ref)
    x1_ref[...] = x1
    u = _rms(x1, mn_ref[...]).astype(BF16)

    p = _dot(u, wqkvg_ref[...])
    q_ref[...] = p[:, :GLA_KEY_WIDTH] * (GLA_DK ** -0.5)
    k_ref[...] = p[:, GLA_KEY_WIDTH:2 * GLA_KEY_WIDTH]
    v_ref[...] = p[:, 2 * GLA_KEY_WIDTH:2 * GLA_KEY_WIDTH + GLA_WIDTH]
    g = p[:, 2 * GLA_KEY_WIDTH + GLA_WIDTH:]
    sg_ref[...] = g * jax.nn.sigmoid(g)

    r = _dot(u, wr_ref[...])
    t = _dot(r.astype(BF16), wgate_ref[...]) + bgate_ref[...]
    log_sig = jnp.minimum(t, 0.0) - jnp.log1p(jnp.exp(-jnp.abs(t)))
    la_ref[...] = log_sig * (1.0 / GATE_TAU)

    pb = _dot(u, wbch_ref[...])
    b_ref[...] = pb[:, :CONV_WIDTH]
    z_ref[...] = pb[:, CONV_WIDTH:2 * CONV_WIDTH] * pb[:, 2 * CONV_WIDTH:]


def _call_a(xp, xs, n1, wg, wu, wd, mn, wqkvg, wr, wbch, wgate, bgate):
    tm = TOKEN_TILE
    n_p, n_s = xp.shape[0] // tm, xs.shape[0] // tm
    t = xp.shape[0] + xs.shape[0]
    row = lambda w: pl.BlockSpec((tm, w), lambda i: (i, 0))
    xp_spec = pl.BlockSpec((tm, D_MODEL), lambda i: (jnp.minimum(i, n_p - 1), 0))
    xs_spec = pl.BlockSpec((tm, D_MODEL), lambda i: (jnp.maximum(i - n_p, 0), 0))
    out_w = (D_MODEL, GLA_KEY_WIDTH, GLA_KEY_WIDTH, GLA_WIDTH, GLA_WIDTH, 2 * GLA_KEY_WIDTH,
             CONV_WIDTH, CONV_WIDTH)
    weights = (n1, wg, wu, wd, mn, wqkvg, wr, wbch, wgate, bgate)
    return pl.pallas_call(
        functools.partial(_kernel_a, n_prompt_tiles=n_p),
        grid=(n_p + n_s,),
        in_specs=[xp_spec, xs_spec] + [_resident(w.shape) for w in weights],
        out_specs=[row(w) for w in out_w],
        out_shape=[jax.ShapeDtypeStruct((t, w), F32) for w in out_w],
        scratch_shapes=[pltpu.VMEM((tm, D_FF), BF16)],
        compiler_params=pltpu.CompilerParams(
            dimension_semantics=("arbitrary",), vmem_limit_bytes=VMEM_LIMIT_BYTES),
        name="ffn1_inproj",
    )(xp, xs, *weights)


_HEAD_BLOCK = [(h * GLA_DK) // LANES for h in range(GLA_HEADS)]
_HEAD_HALF = [(h * GLA_DK) % LANES // GLA_DK for h in range(GLA_HEADS)]


def _gla_decays(jobs):
    rows = GLA_TILE
    n_chunks = rows // CHUNK
    ri = lax.broadcasted_iota(jnp.int32, (rows, rows), 0)
    ci = lax.broadcasted_iota(jnp.int32, (rows, rows), 1)
    same_chunk = (ri // CHUNK) == (ci // CHUNK)
    ctxs = []
    for (q_ref, k_ref, v_ref, la_ref, o_ref, off, reverse) in jobs:
        rs = pl.ds(off, rows)
        keep = same_chunk & ((ci >= ri) if reverse else (ci <= ri))
        tri = jnp.where(keep, 1.0, 0.0).astype(BF16)
        la = la_ref[rs, :]
        la_hi = la.astype(BF16)
        la_lo = (la - la_hi.astype(F32)).astype(BF16)
        cum = _dot(tri, la_hi) + _dot(tri, la_lo)
        totals = [cum[c * CHUNK:c * CHUNK + 1] if reverse else cum[(c + 1) * CHUNK - 1:(c + 1) * CHUNK]
                  for c in range(n_chunks)]
        ctxs.append(dict(keep=keep, cum=cum, totals=totals, o_ref=o_ref, off=off, reverse=reverse))
    for ctx, (q_ref, k_ref, v_ref, _, _, off, _) in zip(ctxs, jobs):
        rs = pl.ds(off, rows)
        cum = ctx["cum"]
        total_rows = jnp.concatenate(
            [jnp.broadcast_to(t, (CHUNK, GLA_KEY_WIDTH)) for t in ctx["totals"]], axis=0)
        k = k_ref[rs, :]
        ctx["qd"] = q_ref[rs, :] * jnp.exp(cum)
        ctx["qd_bf"] = ctx["qd"].astype(BF16)
        ctx["k_inv"] = (k * jnp.exp(-cum)).astype(BF16)
        ctx["k_tail"] = (k * jnp.exp(total_rows - cum)).astype(BF16)
        ctx["v"] = v_ref[rs, :].astype(BF16)
        del ctx["cum"]
    return ctxs


def _gla_products(ctxs):
    n_chunks = GLA_TILE // CHUNK
    lane = lax.broadcasted_iota(jnp.int32, (1, GLA_KEY_WIDTH), 1)
    half = lax.broadcasted_iota(jnp.int32, (1, LANES), 1) // GLA_DK
    for ctx in ctxs:
        ctx["intra"] = []
        ctx["incr"] = []
    for h in range(GLA_HEADS):
        for ctx in ctxs:
            qh = jnp.where((lane // GLA_DK) == h, ctx["qd"], 0.0).astype(BF16)
            s = lax.dot_general(qh, ctx["k_inv"], _NT, preferred_element_type=F32)
            s = jnp.where(ctx["keep"], s, 0.0).astype(BF16)
            ctx["intra"].append(_dot(s, ctx["v"][:, h * GLA_DV:(h + 1) * GLA_DV]))
    for c in range(n_chunks):
        rs = slice(c * CHUNK, (c + 1) * CHUNK)
        for ctx in ctxs:
            full = lax.dot_general(ctx["v"][rs], ctx["k_tail"][rs], _TN, preferred_element_type=F32)
            ctx["incr"].append([
                jnp.where(half == _HEAD_HALF[h],
                          full[h * GLA_DV:(h + 1) * GLA_DV,
                               _HEAD_BLOCK[h] * LANES:(_HEAD_BLOCK[h] + 1) * LANES], 0.0)
                for h in range(GLA_HEADS)])
    for ctx in ctxs:
        del ctx["qd"], ctx["k_inv"], ctx["k_tail"], ctx["v"], ctx["keep"]


def _gla_recurrence(ctxs, states):
    n_chunks = GLA_TILE // CHUNK
    zeros = jnp.zeros((GLA_DV, LANES), BF16)
    states = list(states)
    for step in range(n_chunks):
        for i, ctx in enumerate(ctxs):
            c = n_chunks - 1 - step if ctx["reverse"] else step
            rs = slice(c * CHUNK, (c + 1) * CHUNK)
            st = states[i]
            st_full = jnp.concatenate(
                [jnp.concatenate([st[h].astype(BF16) if b == _HEAD_BLOCK[h] else zeros
                                  for b in range(GLA_KEY_WIDTH // LANES)], axis=1)
                 for h in range(GLA_HEADS)], axis=0)
            inter = lax.dot_general(ctx["qd_bf"][rs], st_full, _NT, preferred_element_type=F32)
            ctx["o_ref"][pl.ds(ctx["off"] + c * CHUNK, CHUNK), :] = (
                jnp.concatenate([x[rs] for x in ctx["intra"]], axis=1) + inter)
            decay = jnp.exp(ctx["totals"][c])
            states[i] = [st[h] * decay[:, _HEAD_BLOCK[h] * LANES:(_HEAD_BLOCK[h] + 1) * LANES]
                         + ctx["incr"][c][h] for h in range(GLA_HEADS)]
    return states


def _kernel_b(qf_ref, kf_ref, vf_ref, laf_ref, qb_ref, kb_ref, vb_ref, lab_ref,
              of_ref, ob_ref, stf_ref, stb_ref, *, layout):
    j = pl.program_id(0)
    n_steps = pl.num_programs(0)
    step_rows = GLA_STEP_TILES * GLA_TILE

    @pl.when(_starts_sequence(j, step_rows, layout))
    def _():
        stf_ref[...] = jnp.zeros_like(stf_ref)

    @pl.when(_ends_sequence(n_steps - 1 - j, step_rows, layout))
    def _():
        stb_ref[...] = jnp.zeros_like(stb_ref)

    def jobs(t):
        return [(qf_ref, kf_ref, vf_ref, laf_ref, of_ref, t * GLA_TILE, False),
                (qb_ref, kb_ref, vb_ref, lab_ref, ob_ref, (GLA_STEP_TILES - 1 - t) * GLA_TILE, True)]

    states = [[ref[h * GLA_DV:(h + 1) * GLA_DV, :] for h in range(GLA_HEADS)]
              for ref in (stf_ref, stb_ref)]
    ctxs = [None] * GLA_STEP_TILES
    ctxs[0] = _gla_decays(jobs(0))
    for t in range(GLA_STEP_TILES):
        if t + 1 < GLA_STEP_TILES:
            ctxs[t + 1] = _gla_decays(jobs(t + 1))
        _gla_products(ctxs[t])
        if t >= 1:
            states = _gla_recurrence(ctxs[t - 1], states)
    states = _gla_recurrence(ctxs[GLA_STEP_TILES - 1], states)
    for ref, st in zip((stf_ref, stb_ref), states):
        for h in range(GLA_HEADS):
            ref[h * GLA_DV:(h + 1) * GLA_DV, :] = st[h]


def _call_b(q, k, v, la, layout):
    t = q.shape[0]
    rows = GLA_STEP_TILES * GLA_TILE
    n = t // rows
    fwd = lambda j: (j, 0)
    bwd = lambda j: (n - 1 - j, 0)
    bwd_la = lambda j: (n - 1 - j, 1)
    kw = GLA_KEY_WIDTH
    spec = lambda w, im: pl.BlockSpec((rows, w), im)
    return pl.pallas_call(
        functools.partial(_kernel_b, layout=layout),
        grid=(n,),
        in_specs=[spec(kw, fwd), spec(kw, fwd), spec(GLA_WIDTH, fwd), spec(kw, fwd),
                  spec(kw, bwd), spec(kw, bwd), spec(GLA_WIDTH, bwd), spec(kw, bwd_la)],
        out_specs=[spec(GLA_WIDTH, fwd), spec(GLA_WIDTH, bwd)],
        out_shape=[jax.ShapeDtypeStruct((t, GLA_WIDTH), F32)] * 2,
        scratch_shapes=[pltpu.VMEM((GLA_WIDTH, LANES), F32)] * 2,
        compiler_params=pltpu.CompilerParams(
            dimension_semantics=("arbitrary",), vmem_limit_bytes=VMEM_LIMIT_BYTES),
        name="gla_bidir",
    )(q, k, v, la, q, k, v, la)


def _kernel_c(x1_ref, of_ref, ob_ref, sg_ref, b_ref, z_ref, zp_ref, zn_ref,
              hn_ref, cw_ref, cn_ref, gm_ref, wo_ref, n2_ref, wg_ref, wu_ref, wd_ref, fn_ref,
              y_ref, h_ref, *, layout, first_tile):
    i = pl.program_id(0) + first_tile
    tm = TOKEN_TILE

    o = of_ref[...] + ob_ref[...]
    heads = []
    for h in range(GLA_HEADS):
        oh = o[:, h * GLA_DV:(h + 1) * GLA_DV]
        heads.append(oh * lax.rsqrt(jnp.mean(oh * oh, axis=-1, keepdims=True) + EPS))
    gla_out = (jnp.concatenate(heads, axis=1) * hn_ref[...]) * sg_ref[...]

    z = z_ref[...]
    row = lax.broadcasted_iota(jnp.int32, (tm, 1), 0)
    prev_row = jnp.where(_starts_sequence(i, tm, layout), 0.0, zp_ref[SUBLANES - 1:SUBLANES, :])
    next_row = jnp.where(_ends_sequence(i, tm, layout), 0.0, zn_ref[0:1, :])
    z_prev = jnp.where(row == 0, prev_row, pltpu.roll(z, 1, axis=0))
    z_next = jnp.where(row == tm - 1, next_row, pltpu.roll(z, tm - 1, axis=0))
    conv = (z_prev * cw_ref[0:1, :] + z * cw_ref[1:2, :]) + z_next * cw_ref[2:3, :]
    yc = b_ref[...] * conv
    sq = yc * yc
    sq_hi = sq.astype(BF16)
    sq_lo = (sq - sq_hi.astype(F32)).astype(BF16)
    gm = gm_ref[...]
    ms = _dot(sq_hi, gm) + _dot(sq_lo, gm)
    conv_out = (yc * lax.rsqrt(ms + EPS)) * cn_ref[...]

    mix = jnp.concatenate([gla_out, conv_out], axis=1).astype(BF16)
    x2 = x1_ref[...] + _dot(mix, wo_ref[...])
    xn = _rms(x2, n2_ref[...]).astype(BF16)
    x3 = x2 + FFN_RESIDUAL * _swiglu(xn, wg_ref, wu_ref, wd_ref, h_ref)
    y_ref[...] = _rms(x3, fn_ref[...])


def _call_c(x1, o_f, o_b, sg, b, z, hn, cw, cn, gm, wo, n2, wg, wu, wd, fn, layout, first_row, n_rows):
    t = x1.shape[0]
    tm = TOKEN_TILE
    n8 = tm // SUBLANES
    t0 = first_row // tm
    row = lambda w: pl.BlockSpec((tm, w), lambda i: (i + t0, 0))
    prev8 = pl.BlockSpec((SUBLANES, CONV_WIDTH), lambda i: (jnp.maximum((i + t0) * n8 - 1, 0), 0))
    next8 = pl.BlockSpec((SUBLANES, CONV_WIDTH),
                         lambda i: (jnp.minimum((i + t0 + 1) * n8, t // SUBLANES - 1), 0))
    weights = (hn, cw, cn, gm, wo, n2, wg, wu, wd, fn)
    return pl.pallas_call(
        functools.partial(_kernel_c, layout=layout, first_tile=t0),
        grid=(n_rows // tm,),
        in_specs=[row(D_MODEL), row(GLA_WIDTH), row(GLA_WIDTH), row(GLA_WIDTH), row(CONV_WIDTH),
                  row(CONV_WIDTH), prev8, next8] + [_resident(w.shape) for w in weights],
        out_specs=pl.BlockSpec((tm, D_MODEL), lambda i: (i, 0)),
        out_shape=jax.ShapeDtypeStruct((n_rows, D_MODEL), F32),
        scratch_shapes=[pltpu.VMEM((tm, D_FF), BF16)],
        compiler_params=pltpu.CompilerParams(
            dimension_semantics=("arbitrary",), vmem_limit_bytes=VMEM_LIMIT_BYTES),
        name="mix_out_ffn2",
    )(x1, o_f, o_b, sg, b, z, z, z, *weights)


def kernel(x_prompt, x_sample, ffn1_norm, ffn1_w_gate, ffn1_w_up, ffn1_w_down, mix_norm, w_in,
           gate_fwd_w, gate_fwd_b, gate_bwd_w, gate_bwd_b, gla_head_norm, conv_w, conv_group_norm,
           w_out, ffn2_norm, ffn2_w_gate, ffn2_w_up, ffn2_w_down, final_norm):
    assert ffn1_norm.shape[0] == 1, "the final norm is fused after the (single) layer"
    l = 0
    o_r = 2 * GLA_KEY_WIDTH + 2 * GLA_WIDTH
    o_b = o_r + 2 * GATE_RANK
    zeros = jnp.zeros((GATE_RANK, GLA_KEY_WIDTH), F32)
    grp = jnp.arange(CONV_WIDTH) // CONV_GROUP_DIM
    group_mean = jnp.where(grp[:, None] == grp[None, :], 1.0 / CONV_GROUP_DIM, 0.0).astype(BF16)
    row = lambda a: a[l].reshape(1, -1)
    wgate = jnp.concatenate(
        [jnp.concatenate([gate_fwd_w[l], zeros], axis=1),
         jnp.concatenate([zeros, gate_bwd_w[l]], axis=1)], axis=0).astype(BF16)
    bgate = jnp.concatenate([gate_fwd_b[l], gate_bwd_b[l]]).reshape(1, -1)

    xp = x_prompt.reshape(-1, D_MODEL)
    xs = x_sample.reshape(-1, D_MODEL)
    layout = (xp.shape[0], x_prompt.shape[1], x_sample.shape[1])
    step_rows = GLA_STEP_TILES * GLA_TILE
    assert all(n % step_rows == 0 and n % TOKEN_TILE == 0 for n in layout)

    x1, q, k, v, sg, la, b, z = _call_a(
        xp, xs, row(ffn1_norm), ffn1_w_gate[l].astype(BF16), ffn1_w_up[l].astype(BF16),
        ffn1_w_down[l].astype(BF16), row(mix_norm), w_in[l, :, :o_r].astype(BF16),
        w_in[l, :, o_r:o_b].astype(BF16), w_in[l, :, o_b:].astype(BF16), wgate, bgate)
    o_f, o_b_ = _call_b(q, k, v, la, layout)
    c_args = (x1, o_f, o_b_, sg, b, z, row(gla_head_norm), conv_w[l], row(conv_group_norm), group_mean,
              w_out[l].astype(BF16), row(ffn2_norm), ffn2_w_gate[l].astype(BF16),
              ffn2_w_up[l].astype(BF16), ffn2_w_down[l].astype(BF16), final_norm.reshape(1, -1), layout)
    yp = _call_c(*c_args, first_row=0, n_rows=xp.shape[0])
    ys = _call_c(*c_args, first_row=xp.shape[0], n_rows=xs.shape[0])
    return yp.reshape(x_prompt.shape), ys.reshape(x_sample.shape)
```

```python
import functools

import jax
import jax.numpy as jnp
from jax import lax
from jax.experimental import pallas as pl
from jax.experimental.pallas import tpu as pltpu

F32 = jnp.float32
BF16 = jnp.bfloat16

D_MODEL = 1024
D_FF = 2816
GLA_WIDTH = 512
CONV_WIDTH = 512
GLA_HEADS = 4
GLA_DV = 128
GLA_KEY_WIDTH = 256
GLA_DK = 64
GATE_RANK = 16
GATE_TAU = 16.0
CHUNK = 64
CONV_GROUP_DIM = 64
FFN_RESIDUAL = 0.5
EPS = 1e-6
LANES = 128
SUBLANES = 8

TOKEN_TILE = 512
FF_CHUNK = 256
GLA_TILE = 256
GLA_STEP_TILES = 4
VMEM_LIMIT_BYTES = 56 * 1024 * 1024

_NT = (((1,), (1,)), ((), ()))
_TN = (((0,), (0,)), ((), ()))


def _dot(a, b):
    return jnp.dot(a, b, preferred_element_type=F32)


def _rms(x, gain):
    inv = lax.rsqrt(jnp.mean(x * x, axis=-1, keepdims=True) + EPS)
    return (x * inv) * gain


def _swiglu(xn, wg_ref, wu_ref, wd_ref, h_ref):
    for c in range(D_FF // FF_CHUNK):
        cols = slice(c * FF_CHUNK, (c + 1) * FF_CHUNK)
        g = _dot(xn, wg_ref[:, cols])
        u = _dot(xn, wu_ref[:, cols])
        h_ref[:, cols] = ((g * jax.nn.sigmoid(g)) * u).astype(BF16)
    return _dot(h_ref[...], wd_ref[...])


def _resident(shape):
    return pl.BlockSpec(shape, lambda *_: (0,) * len(shape), pipeline_mode=pl.Buffered(1))


def _starts_sequence(block, rows, layout):
    prompt_rows, prompt_len, sample_len = layout
    n_prompt = prompt_rows // rows
    in_prompt = block < n_prompt
    return jnp.where(in_prompt, block % (prompt_len // rows) == 0,
                     (block - n_prompt) % (sample_len // rows) == 0)


def _ends_sequence(block, rows, layout):
    prompt_rows, prompt_len, sample_len = layout
    n_prompt = prompt_rows // rows
    in_prompt = block < n_prompt
    return jnp.where(in_prompt, (block + 1) % (prompt_len // rows) == 0,
                     (block - n_prompt + 1) % (sample_len // rows) == 0)


def _kernel_a(xp_ref, xs_ref, n1_ref, wg_ref, wu_ref, wd_ref, mn_ref, wqkvg_ref, wr_ref, wbch_ref,
              wgate_ref, bgate_ref,
              x1_ref, q_ref, k_ref, v_ref, sg_ref, la_ref, b_ref, z_ref, h_ref, *, n_prompt_tiles):
    x = jnp.where(pl.program_id(0) < n_prompt_tiles, xp_ref[...], xs_ref[...])
    xn = _rms(x, n1_ref[...]).astype(BF16)
    x1 = x + FFN_RESIDUAL * _swiglu(xn, wg_ref, wu_ref, wd_ref, h_ref)
    x1_ref[...] = x1
    u = _rms(x1, mn_ref[...]).astype(BF16)

    p = _dot(u, wqkvg_ref[...])
    q_ref[...] = p[:, :GLA_KEY_WIDTH] * (GLA_DK ** -0.5)
    k_ref[...] = p[:, GLA_KEY_WIDTH:2 * GLA_KEY_WIDTH]
    v_ref[...] = p[:, 2 * GLA_KEY_WIDTH:2 * GLA_KEY_WIDTH + GLA_WIDTH]
    g = p[:, 2 * GLA_KEY_WIDTH + GLA_WIDTH:]
    sg_ref[...] = g * jax.nn.sigmoid(g)

    r = _dot(u, wr_ref[...])
    t = _dot(r.astype(BF16), wgate_ref[...]) + bgate_ref[...]
    log_sig = jnp.minimum(t, 0.0) - jnp.log1p(jnp.exp(-jnp.abs(t)))
    la_ref[...] = log_sig * (1.0 / GATE_TAU)

    pb = _dot(u, wbch_ref[...])
    b_ref[...] = pb[:, :CONV_WIDTH]
    z_ref[...] = pb[:, CONV_WIDTH:2 * CONV_WIDTH] * pb[:, 2 * CONV_WIDTH:]


def _call_a(xp, xs, n1, wg, wu, wd, mn, wqkvg, wr, wbch, wgate, bgate):
    tm = TOKEN_TILE
    n_p, n_s = xp.shape[0] // tm, xs.shape[0] // tm
    t = xp.shape[0] + xs.shape[0]
    row = lambda w: pl.BlockSpec((tm, w), lambda i: (i, 0))
    xp_spec = pl.BlockSpec((tm, D_MODEL), lambda i: (jnp.minimum(i, n_p - 1), 0))
    xs_spec = pl.BlockSpec((tm, D_MODEL), lambda i: (jnp.maximum(i - n_p, 0), 0))
    out_w = (D_MODEL, GLA_KEY_WIDTH, GLA_KEY_WIDTH, GLA_WIDTH, GLA_WIDTH, 2 * GLA_KEY_WIDTH,
             CONV_WIDTH, CONV_WIDTH)
    weights = (n1, wg, wu, wd, mn, wqkvg, wr, wbch, wgate, bgate)
    return pl.pallas_call(
        functools.partial(_kernel_a, n_prompt_tiles=n_p),
        grid=(n_p + n_s,),
        in_specs=[xp_spec, xs_spec] + [_resident(w.shape) for w in weights],
        out_specs=[row(w) for w in out_w],
        out_shape=[jax.ShapeDtypeStruct((t, w), F32) for w in out_w],
        scratch_shapes=[pltpu.VMEM((tm, D_FF), BF16)],
        compiler_params=pltpu.CompilerParams(
            dimension_semantics=("arbitrary",), vmem_limit_bytes=VMEM_LIMIT_BYTES),
        name="ffn1_inproj",
    )(xp, xs, *weights)


HEADS_PER_PAIR = LANES // GLA_DK
N_PAIRS = GLA_HEADS // HEADS_PER_PAIR
PAIR_DK = HEADS_PER_PAIR * GLA_DK
PAIR_DV = HEADS_PER_PAIR * GLA_DV


def _gla_decays(jobs):
    rows = GLA_TILE
    n_chunks = rows // CHUNK
    ri = lax.broadcasted_iota(jnp.int32, (rows, rows), 0)
    ci = lax.broadcasted_iota(jnp.int32, (rows, rows), 1)
    same_chunk = (ri // CHUNK) == (ci // CHUNK)
    ctxs = []
    for (q_ref, k_ref, v_ref, la_ref, o_ref, off, reverse) in jobs:
        rs = pl.ds(off, rows)
        keep = same_chunk & ((ci >= ri) if reverse else (ci <= ri))
        tri = jnp.where(keep, 1.0, 0.0).astype(BF16)
        la = la_ref[rs, :]
        la_hi = la.astype(BF16)
        la_lo = (la - la_hi.astype(F32)).astype(BF16)
        cum = _dot(tri, la_hi) + _dot(tri, la_lo)
        totals = [cum[c * CHUNK:c * CHUNK + 1] if reverse else cum[(c + 1) * CHUNK - 1:(c + 1) * CHUNK]
                  for c in range(n_chunks)]
        ctxs.append(dict(keep=keep, cum=cum, totals=totals, o_ref=o_ref, off=off, reverse=reverse))
    for ctx, (q_ref, k_ref, v_ref, _, _, off, _) in zip(ctxs, jobs):
        rs = pl.ds(off, rows)
        cum = ctx.pop("cum")
        totals = ctx.pop("totals")
        total_rows = jnp.concatenate(
            [jnp.broadcast_to(t, (CHUNK, GLA_KEY_WIDTH)) for t in totals], axis=0)
        padded = jnp.concatenate(totals + [jnp.zeros((LANES - n_chunks, GLA_KEY_WIDTH), F32)], axis=0)
        ctx["decay_cols"] = jnp.exp(padded.T)
        k = k_ref[rs, :]
        ctx["qd"] = q_ref[rs, :] * jnp.exp(cum)
        ctx["qd_bf"] = ctx["qd"].astype(BF16)
        ctx["k_inv"] = (k * jnp.exp(-cum)).astype(BF16)
        ctx["k_tail"] = (k * jnp.exp(total_rows - cum)).astype(BF16)
        ctx["v"] = v_ref[rs, :].astype(BF16)
    return ctxs


def _gla_products(ctxs):
    rows = GLA_TILE
    n_chunks = rows // CHUNK
    lane = lax.broadcasted_iota(jnp.int32, (1, GLA_KEY_WIDTH), 1)
    same_head = (lax.broadcasted_iota(jnp.int32, (PAIR_DK, PAIR_DV), 0) // GLA_DK
                 == lax.broadcasted_iota(jnp.int32, (PAIR_DK, PAIR_DV), 1) // GLA_DV)
    for ctx in ctxs:
        q_heads = jnp.concatenate(
            [jnp.where((lane // GLA_DK) == h, ctx["qd"], 0.0) for h in range(GLA_HEADS)], axis=0)
        ctx["s"] = lax.dot_general(q_heads.astype(BF16), ctx["k_inv"], _NT,
                                   preferred_element_type=F32)
    for ctx in ctxs:
        ctx["intra"] = []
    for h in range(GLA_HEADS):
        for ctx in ctxs:
            s = jnp.where(ctx["keep"], ctx["s"][h * rows:(h + 1) * rows], 0.0).astype(BF16)
            ctx["intra"].append(_dot(s, ctx["v"][:, h * GLA_DV:(h + 1) * GLA_DV]))
    for ctx in ctxs:
        ctx["incr"] = []
    for c in range(n_chunks):
        rs = slice(c * CHUNK, (c + 1) * CHUNK)
        for ctx in ctxs:
            ctx["incr"].append([
                jnp.where(same_head,
                          lax.dot_general(ctx["k_tail"][rs, p * PAIR_DK:(p + 1) * PAIR_DK],
                                          ctx["v"][rs, p * PAIR_DV:(p + 1) * PAIR_DV], _TN,
                                          preferred_element_type=F32), 0.0)
                for p in range(N_PAIRS)])
    for ctx in ctxs:
        del ctx["qd"], ctx["k_inv"], ctx["k_tail"], ctx["v"], ctx["keep"], ctx["s"]


def _gla_recurrence(ctxs, states):
    n_chunks = GLA_TILE // CHUNK
    states = list(states)
    for step in range(n_chunks):
        for i, ctx in enumerate(ctxs):
            c = n_chunks - 1 - step if ctx["reverse"] else step
            rs = slice(c * CHUNK, (c + 1) * CHUNK)
            st = states[i]
            inter = jnp.concatenate(
                [_dot(ctx["qd_bf"][rs, p * PAIR_DK:(p + 1) * PAIR_DK], st[p].astype(BF16))
                 for p in range(N_PAIRS)], axis=1)
            ctx["o_ref"][pl.ds(ctx["off"] + c * CHUNK, CHUNK), :] = (
                jnp.concatenate([x[rs] for x in ctx["intra"]], axis=1) + inter)
            states[i] = [st[p] * ctx["decay_cols"][p * PAIR_DK:(p + 1) * PAIR_DK, c:c + 1]
                         + ctx["incr"][c][p] for p in range(N_PAIRS)]
    return states


def _kernel_b(qf_ref, kf_ref, vf_ref, laf_ref, qb_ref, kb_ref, vb_ref, lab_ref,
              of_ref, ob_ref, stf_ref, stb_ref, *, layout):
    j = pl.program_id(0)
    n_steps = pl.num_programs(0)
    step_rows = GLA_STEP_TILES * GLA_TILE

    @pl.when(_starts_sequence(j, step_rows, layout))
    def _():
        stf_ref[...] = jnp.zeros_like(stf_ref)

    @pl.when(_ends_sequence(n_steps - 1 - j, step_rows, layout))
    def _():
        stb_ref[...] = jnp.zeros_like(stb_ref)

    def jobs(t):
        return [(qf_ref, kf_ref, vf_ref, laf_ref, of_ref, t * GLA_TILE, False),
                (qb_ref, kb_ref, vb_ref, lab_ref, ob_ref, (GLA_STEP_TILES - 1 - t) * GLA_TILE, True)]

    states = [[ref[p * PAIR_DK:(p + 1) * PAIR_DK, :] for p in range(N_PAIRS)]
              for ref in (stf_ref, stb_ref)]
    ctxs = [None] * GLA_STEP_TILES
    ctxs[0] = _gla_decays(jobs(0))
    for t in range(GLA_STEP_TILES):
        if t + 1 < GLA_STEP_TILES:
            ctxs[t + 1] = _gla_decays(jobs(t + 1))
        _gla_products(ctxs[t])
        if t >= 1:
            states = _gla_recurrence(ctxs[t - 1], states)
    states = _gla_recurrence(ctxs[GLA_STEP_TILES - 1], states)
    for ref, st in zip((stf_ref, stb_ref), states):
        for p in range(N_PAIRS):
            ref[p * PAIR_DK:(p + 1) * PAIR_DK, :] = st[p]


def _call_b(q, k, v, la, layout):
    t = q.shape[0]
    rows = GLA_STEP_TILES * GLA_TILE
    n = t // rows
    fwd = lambda j: (j, 0)
    bwd = lambda j: (n - 1 - j, 0)
    bwd_la = lambda j: (n - 1 - j, 1)
    kw = GLA_KEY_WIDTH
    spec = lambda w, im: pl.BlockSpec((rows, w), im)
    return pl.pallas_call(
        functools.partial(_kernel_b, layout=layout),
        grid=(n,),
        in_specs=[spec(kw, fwd), spec(kw, fwd), spec(GLA_WIDTH, fwd), spec(kw, fwd),
                  spec(kw, bwd), spec(kw, bwd), spec(GLA_WIDTH, bwd), spec(kw, bwd_la)],
        out_specs=[spec(GLA_WIDTH, fwd), spec(GLA_WIDTH, bwd)],
        out_shape=[jax.ShapeDtypeStruct((t, GLA_WIDTH), F32)] * 2,
        scratch_shapes=[pltpu.VMEM((N_PAIRS * PAIR_DK, PAIR_DV), F32)] * 2,
        compiler_params=pltpu.CompilerParams(
            dimension_semantics=("arbitrary",), vmem_limit_bytes=VMEM_LIMIT_BYTES),
        name="gla_bidir",
    )(q, k, v, la, q, k, v, la)


def _kernel_c(x1_ref, of_ref, ob_ref, sg_ref, b_ref, z_ref, zp_ref, zn_ref,
              hn_ref, cw_ref, cn_ref, gm_ref, wo_ref, n2_ref, wg_ref, wu_ref, wd_ref, fn_ref,
              y_ref, h_ref, *, layout, first_tile):
    i = pl.program_id(0) + first_tile
    tm = TOKEN_TILE

    o = of_ref[...] + ob_ref[...]
    heads = []
    for h in range(GLA_HEADS):
        oh = o[:, h * GLA_DV:(h + 1) * GLA_DV]
        heads.append(oh * lax.rsqrt(jnp.mean(oh * oh, axis=-1, keepdims=True) + EPS))
    gla_out = (jnp.concatenate(heads, axis=1) * hn_ref[...]) * sg_ref[...]

    z = z_ref[...]
    row = lax.broadcasted_iota(jnp.int32, (tm, 1), 0)
    prev_row = jnp.where(_starts_sequence(i, tm, layout), 0.0, zp_ref[SUBLANES - 1:SUBLANES, :])
    next_row = jnp.where(_ends_sequence(i, tm, layout), 0.0, zn_ref[0:1, :])
    z_prev = jnp.where(row == 0, prev_row, pltpu.roll(z, 1, axis=0))
    z_next = jnp.where(row == tm - 1, next_row, pltpu.roll(z, tm - 1, axis=0))
    conv = (z_prev * cw_ref[0:1, :] + z * cw_ref[1:2, :]) + z_next * cw_ref[2:3, :]
    yc = b_ref[...] * conv
    sq = yc * yc
    sq_hi = sq.astype(BF16)
    sq_lo = (sq - sq_hi.astype(F32)).astype(BF16)
    gm = gm_ref[...]
    ms = _dot(sq_hi, gm) + _dot(sq_lo, gm)
    conv_out = (yc * lax.rsqrt(ms + EPS)) * cn_ref[...]

    mix = jnp.concatenate([gla_out, conv_out], axis=1).astype(BF16)
    x2 = x1_ref[...] + _dot(mix, wo_ref[...])
    xn = _rms(x2, n2_ref[...]).astype(BF16)
    x3 = x2 + FFN_RESIDUAL * _swiglu(xn, wg_ref, wu_ref, wd_ref, h_ref)
    y_ref[...] = _rms(x3, fn_ref[...])


def _call_c(x1, o_f, o_b, sg, b, z, hn, cw, cn, gm, wo, n2, wg, wu, wd, fn, layout, first_row, n_rows):
    t = x1.shape[0]
    tm = TOKEN_TILE
    n8 = tm // SUBLANES
    t0 = first_row // tm
    row = lambda w: pl.BlockSpec((tm, w), lambda i: (i + t0, 0))
    prev8 = pl.BlockSpec((SUBLANES, CONV_WIDTH), lambda i: (jnp.maximum((i + t0) * n8 - 1, 0), 0))
    next8 = pl.BlockSpec((SUBLANES, CONV_WIDTH),
                         lambda i: (jnp.minimum((i + t0 + 1) * n8, t // SUBLANES - 1), 0))
    weights = (hn, cw, cn, gm, wo, n2, wg, wu, wd, fn)
    return pl.pallas_call(
        functools.partial(_kernel_c, layout=layout, first_tile=t0),
        grid=(n_rows // tm,),
        in_specs=[row(D_MODEL), row(GLA_WIDTH), row(GLA_WIDTH), row(GLA_WIDTH), row(CONV_WIDTH),
                  row(CONV_WIDTH), prev8, next8] + [_resident(w.shape) for w in weights],
        out_specs=pl.BlockSpec((tm, D_MODEL), lambda i: (i, 0)),
        out_shape=jax.ShapeDtypeStruct((n_rows, D_MODEL), F32),
        scratch_shapes=[pltpu.VMEM((tm, D_FF), BF16)],
        compiler_params=pltpu.CompilerParams(
            dimension_semantics=("arbitrary",), vmem_limit_bytes=VMEM_LIMIT_BYTES),
        name="mix_out_ffn2",
    )(x1, o_f, o_b, sg, b, z, z, z, *weights)


def kernel(x_prompt, x_sample, ffn1_norm, ffn1_w_gate, ffn1_w_up, ffn1_w_down, mix_norm, w_in,
           gate_fwd_w, gate_fwd_b, gate_bwd_w, gate_bwd_b, gla_head_norm, conv_w, conv_group_norm,
           w_out, ffn2_norm, ffn2_w_gate, ffn2_w_up, ffn2_w_down, final_norm):
    assert ffn1_norm.shape[0] == 1, "the final norm is fused after the (single) layer"
    l = 0
    o_r = 2 * GLA_KEY_WIDTH + 2 * GLA_WIDTH
    o_b = o_r + 2 * GATE_RANK
    zeros = jnp.zeros((GATE_RANK, GLA_KEY_WIDTH), F32)
    grp = jnp.arange(CONV_WIDTH) // CONV_GROUP_DIM
    group_mean = jnp.where(grp[:, None] == grp[None, :], 1.0 / CONV_GROUP_DIM, 0.0).astype(BF16)
    row = lambda a: a[l].reshape(1, -1)
    wgate = jnp.concatenate(
        [jnp.concatenate([gate_fwd_w[l], zeros], axis=1),
         jnp.concatenate([zeros, gate_bwd_w[l]], axis=1)], axis=0).astype(BF16)
    bgate = jnp.concatenate([gate_fwd_b[l], gate_bwd_b[l]]).reshape(1, -1)

    xp = x_prompt.reshape(-1, D_MODEL)
    xs = x_sample.reshape(-1, D_MODEL)
    layout = (xp.shape[0], x_prompt.shape[1], x_sample.shape[1])
    step_rows = GLA_STEP_TILES * GLA_TILE
    assert all(n % step_rows == 0 and n % TOKEN_TILE == 0 for n in layout)

    x1, q, k, v, sg, la, b, z = _call_a(
        xp, xs, row(ffn1_norm), ffn1_w_gate[l].astype(BF16), ffn1_w_up[l].astype(BF16),
        ffn1_w_down[l].astype(BF16), row(mix_norm), w_in[l, :, :o_r].astype(BF16),
        w_in[l, :, o_r:o_b].astype(BF16), w_in[l, :, o_b:].astype(BF16), wgate, bgate)
    o_f, o_b_ = _call_b(q, k, v, la, layout)
    c_args = (x1, o_f, o_b_, sg, b, z, row(gla_head_norm), conv_w[l], row(conv_group_norm), group_mean,
              w_out[l].astype(BF16), row(ffn2_norm), ffn2_w_gate[l].astype(BF16),
              ffn2_w_up[l].astype(BF16), ffn2_w_down[l].astype(BF16), final_norm.reshape(1, -1), layout)
    yp = _call_c(*c_args, first_row=0, n_rows=xp.shape[0])
    ys = _call_c(*c_args, first_row=xp.shape[0], n_rows=xs.shape[0])
    return yp.reshape(x_prompt.shape), ys.reshape(x_sample.shape)
```

```python
import functools

import jax
import jax.numpy as jnp
from jax import lax
from jax.experimental import pallas as pl
from jax.experimental.pallas import tpu as pltpu

F32 = jnp.float32
BF16 = jnp.bfloat16

D_MODEL = 1024
D_FF = 2816
GLA_WIDTH = 512
CONV_WIDTH = 512
GLA_HEADS = 4
GLA_DV = 128
GLA_KEY_WIDTH = 256
GLA_DK = 64
GATE_RANK = 16
GATE_TAU = 16.0
CHUNK = 64
CONV_GROUP_DIM = 64
FFN_RESIDUAL = 0.5
EPS = 1e-6
LANES = 128
SUBLANES = 8

TOKEN_TILE = 512
FF_CHUNK = 256
ROW_SPLIT = 2
GLA_TILE = 256
GLA_STEP_TILES = 4
VMEM_LIMIT_BYTES = 56 * 1024 * 1024

_NT = (((1,), (1,)), ((), ()))
_TN = (((0,), (0,)), ((), ()))


def _dot(a, b):
    return jnp.dot(a, b, preferred_element_type=F32)


def _rms(x, gain):
    inv = lax.rsqrt(jnp.mean(x * x, axis=-1, keepdims=True) + EPS)
    return (x * inv) * gain


def _row_parts(rows):
    return [slice(s * rows // ROW_SPLIT, (s + 1) * rows // ROW_SPLIT) for s in range(ROW_SPLIT)]


def _swiglu_hidden(xn_parts, wg_ref, wu_ref, h_ref):
    parts = _row_parts(h_ref.shape[0])
    xn = jnp.concatenate(xn_parts, axis=0)
    for c in range(D_FF // FF_CHUNK):
        cols = slice(c * FF_CHUNK, (c + 1) * FF_CHUNK)
        for rs, lhs in (zip(parts, xn_parts) if c == 0 else [(slice(None), xn)]):
            g = _dot(lhs, wg_ref[:, cols])
            u = _dot(lhs, wu_ref[:, cols])
            h_ref[rs, cols] = ((g * jax.nn.sigmoid(g)) * u).astype(BF16)


def _resident(shape):
    return pl.BlockSpec(shape, lambda *_: (0,) * len(shape), pipeline_mode=pl.Buffered(1))


def _starts_sequence(block, rows, layout):
    prompt_rows, prompt_len, sample_len = layout
    n_prompt = prompt_rows // rows
    in_prompt = block < n_prompt
    return jnp.where(in_prompt, block % (prompt_len // rows) == 0,
                     (block - n_prompt) % (sample_len // rows) == 0)


def _ends_sequence(block, rows, layout):
    prompt_rows, prompt_len, sample_len = layout
    n_prompt = prompt_rows // rows
    in_prompt = block < n_prompt
    return jnp.where(in_prompt, (block + 1) % (prompt_len // rows) == 0,
                     (block - n_prompt + 1) % (sample_len // rows) == 0)


def _kernel_a(xp_ref, xs_ref, n1_ref, wg_ref, wu_ref, wd_ref, mn_ref, wqkvg_ref, wr_ref, wbch_ref,
              wgate_ref, bgate_ref,
              x1_ref, q_ref, k_ref, v_ref, sg_ref, la_ref, b_ref, z_ref, h_ref, *, n_prompt_tiles):
    x = jnp.where(pl.program_id(0) < n_prompt_tiles, xp_ref[...], xs_ref[...])
    halves = _row_parts(TOKEN_TILE)
    _swiglu_hidden([_rms(x[rs], n1_ref[...]).astype(BF16) for rs in halves], wg_ref, wu_ref, h_ref)
    ffn = [_dot(h_ref[rs, :], wd_ref[...]) for rs in halves]
    for rs, f in zip(halves, ffn):
        x1 = x[rs] + FFN_RESIDUAL * f
        x1_ref[rs, :] = x1
        u = _rms(x1, mn_ref[...]).astype(BF16)

        r = _dot(u, wr_ref[...])
        p = _dot(u, wqkvg_ref[...])
        t = _dot(r.astype(BF16), wgate_ref[...]) + bgate_ref[...]
        pb = _dot(u, wbch_ref[...])

        log_sig = jnp.minimum(t, 0.0) - jnp.log(1.0 + jnp.exp(-jnp.abs(t)))
        la_ref[rs, :] = log_sig * (1.0 / GATE_TAU)
        q_ref[rs, :] = p[:, :GLA_KEY_WIDTH] * (GLA_DK ** -0.5)
        k_ref[rs, :] = p[:, GLA_KEY_WIDTH:2 * GLA_KEY_WIDTH]
        v_ref[rs, :] = p[:, 2 * GLA_KEY_WIDTH:2 * GLA_KEY_WIDTH + GLA_WIDTH]
        g = p[:, 2 * GLA_KEY_WIDTH + GLA_WIDTH:]
        sg_ref[rs, :] = g * jax.nn.sigmoid(g)
        b_ref[rs, :] = pb[:, :CONV_WIDTH]
        z_ref[rs, :] = pb[:, CONV_WIDTH:2 * CONV_WIDTH] * pb[:, 2 * CONV_WIDTH:]


def _call_a(xp, xs, n1, wg, wu, wd, mn, wqkvg, wr, wbch, wgate, bgate):
    tm = TOKEN_TILE
    n_p, n_s = xp.shape[0] // tm, xs.shape[0] // tm
    t = xp.shape[0] + xs.shape[0]
    row = lambda w: pl.BlockSpec((tm, w), lambda i: (i, 0))
    xp_spec = pl.BlockSpec((tm, D_MODEL), lambda i: (jnp.minimum(i, n_p - 1), 0))
    xs_spec = pl.BlockSpec((tm, D_MODEL), lambda i: (jnp.maximum(i - n_p, 0), 0))
    out_w = (D_MODEL, GLA_KEY_WIDTH, GLA_KEY_WIDTH, GLA_WIDTH, GLA_WIDTH, 2 * GLA_KEY_WIDTH,
             CONV_WIDTH, CONV_WIDTH)
    weights = (n1, wg, wu, wd, mn, wqkvg, wr, wbch, wgate, bgate)
    return pl.pallas_call(
        functools.partial(_kernel_a, n_prompt_tiles=n_p),
        grid=(n_p + n_s,),
        in_specs=[xp_spec, xs_spec] + [_resident(w.shape) for w in weights],
        out_specs=[row(w) for w in out_w],
        out_shape=[jax.ShapeDtypeStruct((t, w), F32) for w in out_w],
        scratch_shapes=[pltpu.VMEM((tm, D_FF), BF16)],
        compiler_params=pltpu.CompilerParams(
            dimension_semantics=("arbitrary",), vmem_limit_bytes=VMEM_LIMIT_BYTES),
        name="ffn1_inproj",
    )(xp, xs, *weights)


HEADS_PER_PAIR = LANES // GLA_DK
N_PAIRS = GLA_HEADS // HEADS_PER_PAIR
PAIR_DK = HEADS_PER_PAIR * GLA_DK
PAIR_DV = HEADS_PER_PAIR * GLA_DV


def _gla_decays(jobs):
    rows = GLA_TILE
    n_chunks = rows // CHUNK
    ri = lax.broadcasted_iota(jnp.int32, (rows, rows), 0)
    ci = lax.broadcasted_iota(jnp.int32, (rows, rows), 1)
    same_chunk = (ri // CHUNK) == (ci // CHUNK)
    ctxs = []
    for (q_ref, k_ref, v_ref, la_ref, o_ref, off, reverse) in jobs:
        rs = pl.ds(off, rows)
        keep = same_chunk & ((ci >= ri) if reverse else (ci <= ri))
        tri = jnp.where(keep, 1.0, 0.0).astype(BF16)
        la = la_ref[rs, :]
        la_hi = la.astype(BF16)
        la_lo = (la - la_hi.astype(F32)).astype(BF16)
        cum = _dot(tri, la_hi) + _dot(tri, la_lo)
        totals = [cum[c * CHUNK:c * CHUNK + 1] if reverse else cum[(c + 1) * CHUNK - 1:(c + 1) * CHUNK]
                  for c in range(n_chunks)]
        ctxs.append(dict(keep=keep, cum=cum, totals=totals, o_ref=o_ref, off=off, reverse=reverse))
    for ctx, (q_ref, k_ref, v_ref, _, _, off, _) in zip(ctxs, jobs):
        rs = pl.ds(off, rows)
        cum = ctx.pop("cum")
        totals = ctx.pop("totals")
        total_rows = jnp.concatenate(
            [jnp.broadcast_to(t, (CHUNK, GLA_KEY_WIDTH)) for t in totals], axis=0)
        padded = jnp.concatenate(totals + [jnp.zeros((LANES - n_chunks, GLA_KEY_WIDTH), F32)], axis=0)
        ctx["decay_cols"] = jnp.exp(padded.T)
        k = k_ref[rs, :]
        ctx["qd"] = q_ref[rs, :] * jnp.exp(cum)
        ctx["qd_bf"] = ctx["qd"].astype(BF16)
        ctx["k_inv"] = (k * jnp.exp(-cum)).astype(BF16)
        ctx["k_tail"] = (k * jnp.exp(total_rows - cum)).astype(BF16)
        ctx["v"] = v_ref[rs, :].astype(BF16)
    return ctxs


def _gla_products(ctxs):
    rows = GLA_TILE
    n_chunks = rows // CHUNK
    lane = lax.broadcasted_iota(jnp.int32, (1, GLA_KEY_WIDTH), 1)
    same_head = (lax.broadcasted_iota(jnp.int32, (PAIR_DK, PAIR_DV), 0) // GLA_DK
                 == lax.broadcasted_iota(jnp.int32, (PAIR_DK, PAIR_DV), 1) // GLA_DV)
    for ctx in ctxs:
        q_heads = jnp.concatenate(
            [jnp.where((lane // GLA_DK) == h, ctx["qd"], 0.0) for h in range(GLA_HEADS)], axis=0)
        ctx["s"] = lax.dot_general(q_heads.astype(BF16), ctx["k_inv"], _NT,
                                   preferred_element_type=F32)
    for ctx in ctxs:
        ctx["intra"] = []
    for h in range(GLA_HEADS):
        for ctx in ctxs:
            s = jnp.where(ctx["keep"], ctx["s"][h * rows:(h + 1) * rows], 0.0).astype(BF16)
            ctx["intra"].append(_dot(s, ctx["v"][:, h * GLA_DV:(h + 1) * GLA_DV]))
    for ctx in ctxs:
        ctx["incr"] = []
    for c in range(n_chunks):
        rs = slice(c * CHUNK, (c + 1) * CHUNK)
        for ctx in ctxs:
            ctx["incr"].append([
                jnp.where(same_head,
                          lax.dot_general(ctx["k_tail"][rs, p * PAIR_DK:(p + 1) * PAIR_DK],
                                          ctx["v"][rs, p * PAIR_DV:(p + 1) * PAIR_DV], _TN,
                                          preferred_element_type=F32), 0.0)
                for p in range(N_PAIRS)])
    for ctx in ctxs:
        del ctx["qd"], ctx["k_inv"], ctx["k_tail"], ctx["v"], ctx["keep"], ctx["s"]


def _gla_recurrence(ctxs, states):
    n_chunks = GLA_TILE // CHUNK
    states = list(states)
    for step in range(n_chunks):
        for i, ctx in enumerate(ctxs):
            c = n_chunks - 1 - step if ctx["reverse"] else step
            rs = slice(c * CHUNK, (c + 1) * CHUNK)
            st = states[i]
            inter = jnp.concatenate(
                [_dot(ctx["qd_bf"][rs, p * PAIR_DK:(p + 1) * PAIR_DK], st[p].astype(BF16))
                 for p in range(N_PAIRS)], axis=1)
            ctx["o_ref"][pl.ds(ctx["off"] + c * CHUNK, CHUNK), :] = (
                jnp.concatenate([x[rs] for x in ctx["intra"]], axis=1) + inter)
            states[i] = [st[p] * ctx["decay_cols"][p * PAIR_DK:(p + 1) * PAIR_DK, c:c + 1]
                         + ctx["incr"][c][p] for p in range(N_PAIRS)]
    return states


def _kernel_b(*refs, layout, n_cast):
    (qf_ref, kf_ref, vf_ref, laf_ref, qb_ref, kb_ref, vb_ref, lab_ref), refs = refs[:8], refs[8:]
    cast_in, (of_ref, ob_ref), refs = refs[:n_cast], refs[n_cast:n_cast + 2], refs[n_cast + 2:]
    cast_out, (stf_ref, stb_ref) = refs[:n_cast], refs[n_cast:]
    for src, dst in zip(cast_in, cast_out):
        dst[...] = src[...].astype(BF16)
    j = pl.program_id(0)
    n_steps = pl.num_programs(0)
    step_rows = GLA_STEP_TILES * GLA_TILE

    @pl.when(_starts_sequence(j, step_rows, layout))
    def _():
        stf_ref[...] = jnp.zeros_like(stf_ref)

    @pl.when(_ends_sequence(n_steps - 1 - j, step_rows, layout))
    def _():
        stb_ref[...] = jnp.zeros_like(stb_ref)

    def jobs(t):
        return [(qf_ref, kf_ref, vf_ref, laf_ref, of_ref, t * GLA_TILE, False),
                (qb_ref, kb_ref, vb_ref, lab_ref, ob_ref, (GLA_STEP_TILES - 1 - t) * GLA_TILE, True)]

    states = [[ref[p * PAIR_DK:(p + 1) * PAIR_DK, :] for p in range(N_PAIRS)]
              for ref in (stf_ref, stb_ref)]
    ctxs = [None] * GLA_STEP_TILES
    ctxs[0] = _gla_decays(jobs(0))
    for t in range(GLA_STEP_TILES):
        if t + 1 < GLA_STEP_TILES:
            ctxs[t + 1] = _gla_decays(jobs(t + 1))
        _gla_products(ctxs[t])
        if t >= 1:
            states = _gla_recurrence(ctxs[t - 1], states)
    states = _gla_recurrence(ctxs[GLA_STEP_TILES - 1], states)
    for ref, st in zip((stf_ref, stb_ref), states):
        for p in range(N_PAIRS):
            ref[p * PAIR_DK:(p + 1) * PAIR_DK, :] = st[p]


def _cast_slab_spec(shape, n_steps):
    bf16_rows = 2 * SUBLANES
    d = max(d for d in range(1, n_steps + 1)
            if n_steps % d == 0 and shape[0] % d == 0 and (shape[0] // d) % bf16_rows == 0)
    return pl.BlockSpec((shape[0] // d, shape[1]), lambda j: (jnp.minimum(j, d - 1), 0))


def _call_b(q, k, v, la, layout, cast_weights):
    t = q.shape[0]
    rows = GLA_STEP_TILES * GLA_TILE
    n = t // rows
    fwd = lambda j: (j, 0)
    bwd = lambda j: (n - 1 - j, 0)
    bwd_la = lambda j: (n - 1 - j, 1)
    kw = GLA_KEY_WIDTH
    spec = lambda w, im: pl.BlockSpec((rows, w), im)
    cast_specs = [_cast_slab_spec(w.shape, n) for w in cast_weights]
    return pl.pallas_call(
        functools.partial(_kernel_b, layout=layout, n_cast=len(cast_weights)),
        grid=(n,),
        in_specs=[spec(kw, fwd), spec(kw, fwd), spec(GLA_WIDTH, fwd), spec(kw, fwd),
                  spec(kw, bwd), spec(kw, bwd), spec(GLA_WIDTH, bwd), spec(kw, bwd_la)] + cast_specs,
        out_specs=[spec(GLA_WIDTH, fwd), spec(GLA_WIDTH, bwd)] + cast_specs,
        out_shape=[jax.ShapeDtypeStruct((t, GLA_WIDTH), F32)] * 2
        + [jax.ShapeDtypeStruct(w.shape, BF16) for w in cast_weights],
        scratch_shapes=[pltpu.VMEM((N_PAIRS * PAIR_DK, PAIR_DV), F32)] * 2,
        compiler_params=pltpu.CompilerParams(
            dimension_semantics=("arbitrary",), vmem_limit_bytes=VMEM_LIMIT_BYTES),
        name="gla_bidir",
    )(q, k, v, la, q, k, v, la, *cast_weights)


def _group_inv_rms(y):
    groups_per_block = LANES // CONV_GROUP_DIM
    lane_group = lax.broadcasted_iota(jnp.int32, (1, LANES), 1) // CONV_GROUP_DIM
    out = []
    for j in range(CONV_WIDTH // LANES):
        blk = y[:, j * LANES:(j + 1) * LANES]
        sq = blk * blk
        inv = None
        for g in range(groups_per_block):
            total = jnp.sum(jnp.where(lane_group == g, sq, 0.0), axis=-1, keepdims=True)
            inv_g = lax.rsqrt(total * (1.0 / CONV_GROUP_DIM) + EPS)
            inv = inv_g if inv is None else jnp.where(lane_group == g, inv_g, inv)
        out.append(jnp.broadcast_to(inv, blk.shape))
    return jnp.concatenate(out, axis=1)


def _kernel_c(x1_ref, of_ref, ob_ref, sg_ref, b_ref, z_ref, zp_ref, zn_ref,
              hn_ref, cw_ref, cn_ref, wo_ref, n2_ref, wg_ref, wu_ref, wd_ref, fn_ref,
              y_ref, h_ref, *, layout, first_tile):
    i = pl.program_id(0) + first_tile
    tm = TOKEN_TILE

    o = of_ref[...] + ob_ref[...]
    heads = []
    for h in range(GLA_HEADS):
        oh = o[:, h * GLA_DV:(h + 1) * GLA_DV]
        heads.append(oh * lax.rsqrt(jnp.mean(oh * oh, axis=-1, keepdims=True) + EPS))
    gla_out = (jnp.concatenate(heads, axis=1) * hn_ref[...]) * sg_ref[...]

    z = z_ref[...]
    row = lax.broadcasted_iota(jnp.int32, (tm, 1), 0)
    prev_row = jnp.where(_starts_sequence(i, tm, layout), 0.0, zp_ref[SUBLANES - 1:SUBLANES, :])
    next_row = jnp.where(_ends_sequence(i, tm, layout), 0.0, zn_ref[0:1, :])
    z_prev = jnp.where(row == 0, prev_row, pltpu.roll(z, 1, axis=0))
    z_next = jnp.where(row == tm - 1, next_row, pltpu.roll(z, tm - 1, axis=0))
    conv = (z_prev * cw_ref[0:1, :] + z * cw_ref[1:2, :]) + z_next * cw_ref[2:3, :]
    yc = b_ref[...] * conv
    conv_out = (yc * _group_inv_rms(yc)) * cn_ref[...]

    mix = jnp.concatenate([gla_out, conv_out], axis=1).astype(BF16)
    halves = _row_parts(tm)
    x2 = [x1_ref[rs, :] + _dot(mix[rs], wo_ref[...]) for rs in halves]
    _swiglu_hidden([_rms(x, n2_ref[...]).astype(BF16) for x in x2], wg_ref, wu_ref, h_ref)
    ffn = [_dot(h_ref[rs, :], wd_ref[...]) for rs in halves]
    for rs, x, f in zip(halves, x2, ffn):
        y_ref[rs, :] = _rms(x + FFN_RESIDUAL * f, fn_ref[...])


def _call_c(x1, o_f, o_b, sg, b, z, hn, cw, cn, wo, n2, wg, wu, wd, fn, layout, first_row, n_rows):
    t = x1.shape[0]
    tm = TOKEN_TILE
    n8 = tm // SUBLANES
    t0 = first_row // tm
    row = lambda w: pl.BlockSpec((tm, w), lambda i: (i + t0, 0))
    prev8 = pl.BlockSpec((SUBLANES, CONV_WIDTH), lambda i: (jnp.maximum((i + t0) * n8 - 1, 0), 0))
    next8 = pl.BlockSpec((SUBLANES, CONV_WIDTH),
                         lambda i: (jnp.minimum((i + t0 + 1) * n8, t // SUBLANES - 1), 0))
    weights = (hn, cw, cn, wo, n2, wg, wu, wd, fn)
    return pl.pallas_call(
        functools.partial(_kernel_c, layout=layout, first_tile=t0),
        grid=(n_rows // tm,),
        in_specs=[row(D_MODEL), row(GLA_WIDTH), row(GLA_WIDTH), row(GLA_WIDTH), row(CONV_WIDTH),
                  row(CONV_WIDTH), prev8, next8] + [_resident(w.shape) for w in weights],
        out_specs=pl.BlockSpec((tm, D_MODEL), lambda i: (i, 0)),
        out_shape=jax.ShapeDtypeStruct((n_rows, D_MODEL), F32),
        scratch_shapes=[pltpu.VMEM((tm, D_FF), BF16)],
        compiler_params=pltpu.CompilerParams(
            dimension_semantics=("arbitrary",), vmem_limit_bytes=VMEM_LIMIT_BYTES),
        name="mix_out_ffn2",
    )(x1, o_f, o_b, sg, b, z, z, z, *weights)


def kernel(x_prompt, x_sample, ffn1_norm, ffn1_w_gate, ffn1_w_up, ffn1_w_down, mix_norm, w_in,
           gate_fwd_w, gate_fwd_b, gate_bwd_w, gate_bwd_b, gla_head_norm, conv_w, conv_group_norm,
           w_out, ffn2_norm, ffn2_w_gate, ffn2_w_up, ffn2_w_down, final_norm):
    assert ffn1_norm.shape[0] == 1, "the final norm is fused after the (single) layer"
    l = 0
    o_r = 2 * GLA_KEY_WIDTH + 2 * GLA_WIDTH
    o_b = o_r + 2 * GATE_RANK
    zeros = jnp.zeros((GATE_RANK, GLA_KEY_WIDTH), F32)
    row = lambda a: a[l].reshape(1, -1)
    wgate = jnp.concatenate(
        [jnp.concatenate([gate_fwd_w[l], zeros], axis=1),
         jnp.concatenate([zeros, gate_bwd_w[l]], axis=1)], axis=0).astype(BF16)
    bgate = jnp.concatenate([gate_fwd_b[l], gate_bwd_b[l]]).reshape(1, -1)

    xp = x_prompt.reshape(-1, D_MODEL)
    xs = x_sample.reshape(-1, D_MODEL)
    layout = (xp.shape[0], x_prompt.shape[1], x_sample.shape[1])
    step_rows = GLA_STEP_TILES * GLA_TILE
    assert all(n % step_rows == 0 and n % TOKEN_TILE == 0 for n in layout)

    x1, q, k, v, sg, la, b, z = _call_a(
        xp, xs, row(ffn1_norm), ffn1_w_gate[l].astype(BF16), ffn1_w_up[l].astype(BF16),
        ffn1_w_down[l].astype(BF16), row(mix_norm), w_in[l, :, :o_r].astype(BF16),
        w_in[l, :, o_r:o_b].astype(BF16), w_in[l, :, o_b:].astype(BF16), wgate, bgate)
    o_f, o_b_, wo, wg2, wu2, wd2 = _call_b(
        q, k, v, la, layout, (w_out[l], ffn2_w_gate[l], ffn2_w_up[l], ffn2_w_down[l]))
    c_args = (x1, o_f, o_b_, sg, b, z, row(gla_head_norm), conv_w[l], row(conv_group_norm),
              wo, row(ffn2_norm), wg2, wu2, wd2, final_norm.reshape(1, -1), layout)
    yp = _call_c(*c_args, first_row=0, n_rows=xp.shape[0])
    ys = _call_c(*c_args, first_row=xp.shape[0], n_rows=xs.shape[0])
    return yp.reshape(x_prompt.shape), ys.reshape(x_sample.shape)
```

```python
import functools

import jax
import jax.numpy as jnp
from jax import lax
from jax.experimental import pallas as pl
from jax.experimental.pallas import tpu as pltpu

F32 = jnp.float32
BF16 = jnp.bfloat16

D_MODEL = 1024
D_FF = 2816
GLA_WIDTH = 512
CONV_WIDTH = 512
GLA_HEADS = 4
GLA_DV = 128
GLA_KEY_WIDTH = 256
GLA_DK = 64
GATE_RANK = 16
GATE_TAU = 16.0
CHUNK = 64
CONV_GROUP_DIM = 64
FFN_RESIDUAL = 0.5
EPS = 1e-6
LANES = 128
SUBLANES = 8

TOKEN_TILE = 512
FF_CHUNK = 256
ROW_SPLIT = 2
GLA_TILE = 256
GLA_STEP_TILES = 4
VMEM_LIMIT_BYTES = 56 * 1024 * 1024

_NT = (((1,), (1,)), ((), ()))
_TN = (((0,), (0,)), ((), ()))


def _dot(a, b):
    return jnp.dot(a, b, preferred_element_type=F32)


def _rms(x, gain):
    inv = lax.rsqrt(jnp.mean(x * x, axis=-1, keepdims=True) + EPS)
    return (x * inv) * gain


def _row_parts(rows):
    return [slice(s * rows // ROW_SPLIT, (s + 1) * rows // ROW_SPLIT) for s in range(ROW_SPLIT)]


def _swiglu_hidden(xn_parts, wg_ref, wu_ref, h_ref, side_work=()):
    parts = _row_parts(h_ref.shape[0])
    xn = jnp.concatenate(xn_parts, axis=0)
    side_work = list(side_work)
    assert len(side_work) <= D_FF // FF_CHUNK
    for c in range(D_FF // FF_CHUNK):
        cols = slice(c * FF_CHUNK, (c + 1) * FF_CHUNK)
        for rs, lhs in (zip(parts, xn_parts) if c == 0 else [(slice(None), xn)]):
            g = _dot(lhs, wg_ref[:, cols])
            u = _dot(lhs, wu_ref[:, cols])
            h_ref[rs, cols] = ((g * jax.nn.sigmoid(g)) * u).astype(BF16)
        if c < len(side_work):
            side_work[c]()


def _resident(shape):
    return pl.BlockSpec(shape, lambda *_: (0,) * len(shape), pipeline_mode=pl.Buffered(1))


def _starts_sequence(block, rows, layout):
    prompt_rows, prompt_len, sample_len = layout
    n_prompt = prompt_rows // rows
    in_prompt = block < n_prompt
    return jnp.where(in_prompt, block % (prompt_len // rows) == 0,
                     (block - n_prompt) % (sample_len // rows) == 0)


def _ends_sequence(block, rows, layout):
    prompt_rows, prompt_len, sample_len = layout
    n_prompt = prompt_rows // rows
    in_prompt = block < n_prompt
    return jnp.where(in_prompt, (block + 1) % (prompt_len // rows) == 0,
                     (block - n_prompt + 1) % (sample_len // rows) == 0)


def _kernel_a(xp_ref, xs_ref, n1_ref, wg_ref, wu_ref, wd_ref, mn_ref, win_ref, wgate_ref, bgate_ref,
              x1_ref, q_ref, k_ref, v_ref, sg_ref, la_ref, b_ref, z_ref, h_ref, *, n_prompt_tiles):
    n_qkvg = 2 * GLA_KEY_WIDTH + 2 * GLA_WIDTH
    n_bch = 3 * CONV_WIDTH
    x = jnp.where(pl.program_id(0) < n_prompt_tiles, xp_ref[...], xs_ref[...])
    halves = _row_parts(TOKEN_TILE)
    _swiglu_hidden([_rms(x[rs], n1_ref[...]).astype(BF16) for rs in halves], wg_ref, wu_ref, h_ref)
    ffn = [_dot(h_ref[rs, :], wd_ref[...]) for rs in halves]
    for rs, f in zip(halves, ffn):
        x1 = x[rs] + FFN_RESIDUAL * f
        x1_ref[rs, :] = x1
        u = _rms(x1, mn_ref[...]).astype(BF16)

        r = _dot(u, win_ref[:, n_qkvg + n_bch:])
        p = _dot(u, win_ref[:, :n_qkvg])
        t = _dot(r.astype(BF16), wgate_ref[...]) + bgate_ref[...]
        pb = _dot(u, win_ref[:, n_qkvg:n_qkvg + n_bch])

        log_sig = jnp.minimum(t, 0.0) - jnp.log(1.0 + jnp.exp(-jnp.abs(t)))
        la_ref[rs, :] = log_sig * (1.0 / GATE_TAU)
        q_ref[rs, :] = p[:, :GLA_KEY_WIDTH] * (GLA_DK ** -0.5)
        k_ref[rs, :] = p[:, GLA_KEY_WIDTH:2 * GLA_KEY_WIDTH]
        v_ref[rs, :] = p[:, 2 * GLA_KEY_WIDTH:2 * GLA_KEY_WIDTH + GLA_WIDTH].astype(BF16)
        g = p[:, 2 * GLA_KEY_WIDTH + GLA_WIDTH:]
        sg_ref[rs, :] = g * jax.nn.sigmoid(g)
        b_ref[rs, :] = pb[:, :CONV_WIDTH]
        z_ref[rs, :] = pb[:, CONV_WIDTH:2 * CONV_WIDTH] * pb[:, 2 * CONV_WIDTH:]


def _call_a(xp, xs, n1, wg, wu, wd, mn, win, wgate, bgate):
    tm = TOKEN_TILE
    n_p, n_s = xp.shape[0] // tm, xs.shape[0] // tm
    t = xp.shape[0] + xs.shape[0]
    row = lambda w: pl.BlockSpec((tm, w), lambda i: (i, 0))
    xp_spec = pl.BlockSpec((tm, D_MODEL), lambda i: (jnp.minimum(i, n_p - 1), 0))
    xs_spec = pl.BlockSpec((tm, D_MODEL), lambda i: (jnp.maximum(i - n_p, 0), 0))
    out_w = (D_MODEL, GLA_KEY_WIDTH, GLA_KEY_WIDTH, GLA_WIDTH, GLA_WIDTH, 2 * GLA_KEY_WIDTH,
             CONV_WIDTH, CONV_WIDTH)
    out_dtype = (F32, F32, F32, BF16, F32, F32, F32, F32)
    weights = (n1, wg, wu, wd, mn, win, wgate, bgate)
    return pl.pallas_call(
        functools.partial(_kernel_a, n_prompt_tiles=n_p),
        grid=(n_p + n_s,),
        in_specs=[xp_spec, xs_spec] + [_resident(w.shape) for w in weights],
        out_specs=[row(w) for w in out_w],
        out_shape=[jax.ShapeDtypeStruct((t, w), dt) for w, dt in zip(out_w, out_dtype)],
        scratch_shapes=[pltpu.VMEM((tm, D_FF), BF16)],
        compiler_params=pltpu.CompilerParams(
            dimension_semantics=("arbitrary",), vmem_limit_bytes=VMEM_LIMIT_BYTES),
        name="ffn1_inproj",
    )(xp, xs, *weights)


HEADS_PER_PAIR = LANES // GLA_DK
N_PAIRS = GLA_HEADS // HEADS_PER_PAIR
PAIR_DK = HEADS_PER_PAIR * GLA_DK
PAIR_DV = HEADS_PER_PAIR * GLA_DV


def _gla_decays(jobs):
    rows = GLA_TILE
    n_chunks = rows // CHUNK
    ri = lax.broadcasted_iota(jnp.int32, (rows, rows), 0)
    ci = lax.broadcasted_iota(jnp.int32, (rows, rows), 1)
    same_chunk = (ri // CHUNK) == (ci // CHUNK)
    ctxs = []
    for (q_ref, k_ref, v_ref, la_ref, o_ref, off, reverse) in jobs:
        rs = pl.ds(off, rows)
        keep = same_chunk & ((ci >= ri) if reverse else (ci <= ri))
        tri = jnp.where(keep, 1.0, 0.0).astype(BF16)
        la = la_ref[rs, :]
        la_hi = la.astype(BF16)
        la_lo = (la - la_hi.astype(F32)).astype(BF16)
        cum = _dot(tri, la_hi) + _dot(tri, la_lo)
        totals = [cum[c * CHUNK:c * CHUNK + 1] if reverse else cum[(c + 1) * CHUNK - 1:(c + 1) * CHUNK]
                  for c in range(n_chunks)]
        ctxs.append(dict(keep=keep, cum=cum, totals=totals, o_ref=o_ref, off=off, reverse=reverse))
    for ctx, (q_ref, k_ref, v_ref, _, _, off, _) in zip(ctxs, jobs):
        rs = pl.ds(off, rows)
        cum = ctx.pop("cum")
        totals = ctx.pop("totals")
        total_rows = jnp.concatenate(
            [jnp.broadcast_to(t, (CHUNK, GLA_KEY_WIDTH)) for t in totals], axis=0)
        padded = jnp.concatenate(totals + [jnp.zeros((LANES - n_chunks, GLA_KEY_WIDTH), F32)], axis=0)
        ctx["decay_cols"] = jnp.exp(padded.T)
        k = k_ref[rs, :]
        ctx["qd"] = q_ref[rs, :] * jnp.exp(cum)
        ctx["qd_bf"] = ctx["qd"].astype(BF16)
        ctx["k_inv"] = (k * jnp.exp(-cum)).astype(BF16)
        ctx["k_tail"] = (k * jnp.exp(total_rows - cum)).astype(BF16)
        ctx["v"] = v_ref[rs, :]
    return ctxs


def _gla_products(ctxs):
    rows = GLA_TILE
    n_chunks = rows // CHUNK
    lane = lax.broadcasted_iota(jnp.int32, (1, GLA_KEY_WIDTH), 1)
    same_head = (lax.broadcasted_iota(jnp.int32, (PAIR_DK, PAIR_DV), 0) // GLA_DK
                 == lax.broadcasted_iota(jnp.int32, (PAIR_DK, PAIR_DV), 1) // GLA_DV)
    for ctx in ctxs:
        q_heads = jnp.concatenate(
            [jnp.where((lane // GLA_DK) == h, ctx["qd"], 0.0) for h in range(GLA_HEADS)], axis=0)
        ctx["s"] = lax.dot_general(q_heads.astype(BF16), ctx["k_inv"], _NT,
                                   preferred_element_type=F32)
    for ctx in ctxs:
        ctx["intra"] = []
    for h in range(GLA_HEADS):
        for ctx in ctxs:
            s = jnp.where(ctx["keep"], ctx["s"][h * rows:(h + 1) * rows], 0.0).astype(BF16)
            ctx["intra"].append(_dot(s, ctx["v"][:, h * GLA_DV:(h + 1) * GLA_DV]))
    for ctx in ctxs:
        ctx["incr"] = []
    for c in range(n_chunks):
        rs = slice(c * CHUNK, (c + 1) * CHUNK)
        for ctx in ctxs:
            ctx["incr"].append([
                jnp.where(same_head,
                          lax.dot_general(ctx["k_tail"][rs, p * PAIR_DK:(p + 1) * PAIR_DK],
                                          ctx["v"][rs, p * PAIR_DV:(p + 1) * PAIR_DV], _TN,
                                          preferred_element_type=F32), 0.0)
                for p in range(N_PAIRS)])
    for ctx in ctxs:
        del ctx["qd"], ctx["k_inv"], ctx["k_tail"], ctx["v"], ctx["keep"], ctx["s"]


def _gla_recurrence(ctxs, states):
    n_chunks = GLA_TILE // CHUNK
    states = list(states)
    for step in range(n_chunks):
        for i, ctx in enumerate(ctxs):
            c = n_chunks - 1 - step if ctx["reverse"] else step
            rs = slice(c * CHUNK, (c + 1) * CHUNK)
            st = states[i]
            inter = jnp.concatenate(
                [_dot(ctx["qd_bf"][rs, p * PAIR_DK:(p + 1) * PAIR_DK], st[p].astype(BF16))
                 for p in range(N_PAIRS)], axis=1)
            ctx["o_ref"][pl.ds(ctx["off"] + c * CHUNK, CHUNK), :] = (
                jnp.concatenate([x[rs] for x in ctx["intra"]], axis=1) + inter)
            states[i] = [st[p] * ctx["decay_cols"][p * PAIR_DK:(p + 1) * PAIR_DK, c:c + 1]
                         + ctx["incr"][c][p] for p in range(N_PAIRS)]
    return states


def _kernel_b(*refs, layout, n_cast):
    (qf_ref, kf_ref, vf_ref, laf_ref, qb_ref, kb_ref, vb_ref, lab_ref), refs = refs[:8], refs[8:]
    cast_in, (of_ref, ob_ref), refs = refs[:n_cast], refs[n_cast:n_cast + 2], refs[n_cast + 2:]
    cast_out, (stf_ref, stb_ref) = refs[:n_cast], refs[n_cast:]
    for src, dst in zip(cast_in, cast_out):
        dst[...] = src[...].astype(BF16)
    j = pl.program_id(0)
    n_steps = pl.num_programs(0)
    step_rows = GLA_STEP_TILES * GLA_TILE

    @pl.when(_starts_sequence(j, step_rows, layout))
    def _():
        stf_ref[...] = jnp.zeros_like(stf_ref)

    @pl.when(_ends_sequence(n_steps - 1 - j, step_rows, layout))
    def _():
        stb_ref[...] = jnp.zeros_like(stb_ref)

    def jobs(t):
        return [(qf_ref, kf_ref, vf_ref, laf_ref, of_ref, t * GLA_TILE, False),
                (qb_ref, kb_ref, vb_ref, lab_ref, ob_ref, (GLA_STEP_TILES - 1 - t) * GLA_TILE, True)]

    states = [[ref[p * PAIR_DK:(p + 1) * PAIR_DK, :] for p in range(N_PAIRS)]
              for ref in (stf_ref, stb_ref)]
    ctxs = [None] * GLA_STEP_TILES
    ctxs[0] = _gla_decays(jobs(0))
    for t in range(GLA_STEP_TILES):
        if t + 1 < GLA_STEP_TILES:
            ctxs[t + 1] = _gla_decays(jobs(t + 1))
        _gla_products(ctxs[t])
        if t >= 1:
            states = _gla_recurrence(ctxs[t - 1], states)
    states = _gla_recurrence(ctxs[GLA_STEP_TILES - 1], states)
    for ref, st in zip((stf_ref, stb_ref), states):
        for p in range(N_PAIRS):
            ref[p * PAIR_DK:(p + 1) * PAIR_DK, :] = st[p]


def _cast_slab_spec(shape, n_steps):
    bf16_rows = 2 * SUBLANES
    d = max(d for d in range(1, n_steps + 1)
            if n_steps % d == 0 and shape[0] % d == 0 and (shape[0] // d) % bf16_rows == 0)
    return pl.BlockSpec((shape[0] // d, shape[1]), lambda j: (jnp.minimum(j, d - 1), 0))


def _call_b(q, k, v, la, layout, cast_weights):
    t = q.shape[0]
    rows = GLA_STEP_TILES * GLA_TILE
    n = t // rows
    fwd = lambda j: (j, 0)
    bwd = lambda j: (n - 1 - j, 0)
    bwd_la = lambda j: (n - 1 - j, 1)
    kw = GLA_KEY_WIDTH
    spec = lambda w, im: pl.BlockSpec((rows, w), im)
    cast_specs = [_cast_slab_spec(w.shape, n) for w in cast_weights]
    return pl.pallas_call(
        functools.partial(_kernel_b, layout=layout, n_cast=len(cast_weights)),
        grid=(n,),
        in_specs=[spec(kw, fwd), spec(kw, fwd), spec(GLA_WIDTH, fwd), spec(kw, fwd),
                  spec(kw, bwd), spec(kw, bwd), spec(GLA_WIDTH, bwd), spec(kw, bwd_la)] + cast_specs,
        out_specs=[spec(GLA_WIDTH, fwd), spec(GLA_WIDTH, bwd)] + cast_specs,
        out_shape=[jax.ShapeDtypeStruct((t, GLA_WIDTH), F32)] * 2
        + [jax.ShapeDtypeStruct(w.shape, BF16) for w in cast_weights],
        scratch_shapes=[pltpu.VMEM((N_PAIRS * PAIR_DK, PAIR_DV), F32)] * 2,
        compiler_params=pltpu.CompilerParams(
            dimension_semantics=("arbitrary",), vmem_limit_bytes=VMEM_LIMIT_BYTES),
        name="gla_bidir",
    )(q, k, v, la, q, k, v, la, *cast_weights)


def _group_inv_rms(y):
    groups_per_block = LANES // CONV_GROUP_DIM
    lane_group = lax.broadcasted_iota(jnp.int32, (1, LANES), 1) // CONV_GROUP_DIM
    out = []
    for j in range(CONV_WIDTH // LANES):
        blk = y[:, j * LANES:(j + 1) * LANES]
        sq = blk * blk
        total = None
        for g in range(groups_per_block):
            total_g = jnp.sum(jnp.where(lane_group == g, sq, 0.0), axis=-1, keepdims=True)
            total = total_g if total is None else jnp.where(lane_group == g, total_g, total)
        out.append(jnp.broadcast_to(lax.rsqrt(total * (1.0 / CONV_GROUP_DIM) + EPS), blk.shape))
    return jnp.concatenate(out, axis=1)


def _mixer_input(r0, r1, tile, layout, of_ref, ob_ref, sg_ref, b_ref, z_ref, zp_ref, zn_ref,
                 hn_ref, cw_ref, cn_ref):
    tm = TOKEN_TILE
    n = r1 - r0
    o = of_ref[r0:r1, :] + ob_ref[r0:r1, :]
    heads = []
    for h in range(GLA_HEADS):
        oh = o[:, h * GLA_DV:(h + 1) * GLA_DV]
        heads.append(oh * lax.rsqrt(jnp.mean(oh * oh, axis=-1, keepdims=True) + EPS))
    gla_out = (jnp.concatenate(heads, axis=1) * hn_ref[...]) * sg_ref[r0:r1, :]

    above = (jnp.where(_starts_sequence(tile, tm, layout), 0.0, zp_ref[...]) if r0 == 0
             else z_ref[r0 - SUBLANES:r0, :])
    below = (jnp.where(_ends_sequence(tile, tm, layout), 0.0, zn_ref[...]) if r1 == tm
             else z_ref[r1:r1 + SUBLANES, :])
    slab = jnp.concatenate([above, z_ref[r0:r1, :], below], axis=0)
    inner = slice(SUBLANES, SUBLANES + n)
    z_prev = pltpu.roll(slab, 1, axis=0)[inner]
    z_next = pltpu.roll(slab, n + 2 * SUBLANES - 1, axis=0)[inner]
    conv = (z_prev * cw_ref[0:1, :] + slab[inner] * cw_ref[1:2, :]) + z_next * cw_ref[2:3, :]
    yc = b_ref[r0:r1, :] * conv
    conv_out = (yc * _group_inv_rms(yc)) * cn_ref[...]
    return jnp.concatenate([gla_out, conv_out], axis=1).astype(BF16)


N_MIX_STREAMS = 7
MIX_ROW_CHUNK = 64


def _kernel_c(x1_ref, *refs, layout, first_tile, n_tiles):
    next_refs, first_refs, refs = (refs[:N_MIX_STREAMS], refs[N_MIX_STREAMS:2 * N_MIX_STREAMS],
                                   refs[2 * N_MIX_STREAMS:])
    (hn_ref, cw_ref, cn_ref, wo_ref, n2_ref, wg_ref, wu_ref, wd_ref, fn_ref,
     y_ref, h_ref, mix_ref, mix_next_ref) = refs
    i = pl.program_id(0)
    tm = TOKEN_TILE
    row_chunks = [(r, r + MIX_ROW_CHUNK) for r in range(0, tm, MIX_ROW_CHUNK)]

    @pl.when(i == 0)
    def _():
        for r0, r1 in row_chunks:
            mix_ref[r0:r1, :] = _mixer_input(r0, r1, first_tile, layout, *first_refs, hn_ref, cw_ref, cn_ref)

    next_tile = first_tile + jnp.minimum(i + 1, n_tiles - 1)

    def next_chunk(r0, r1):
        mix_next_ref[r0:r1, :] = _mixer_input(r0, r1, next_tile, layout, *next_refs, hn_ref, cw_ref, cn_ref)

    halves = _row_parts(tm)
    x2 = [x1_ref[rs, :] + _dot(mix_ref[rs, :], wo_ref[...]) for rs in halves]
    _swiglu_hidden([_rms(x, n2_ref[...]).astype(BF16) for x in x2], wg_ref, wu_ref, h_ref,
                   side_work=[functools.partial(next_chunk, r0, r1) for r0, r1 in row_chunks])
    ffn = [_dot(h_ref[rs, :], wd_ref[...]) for rs in halves]
    for rs, x, f in zip(halves, x2, ffn):
        y_ref[rs, :] = _rms(x + FFN_RESIDUAL * f, fn_ref[...])
    mix_ref[...] = mix_next_ref[...]


def _call_c(x1, o_f, o_b, sg, b, z, hn, cw, cn, wo, n2, wg, wu, wd, fn, layout, first_row, n_rows):
    t = x1.shape[0]
    tm = TOKEN_TILE
    n8 = tm // SUBLANES
    t0 = first_row // tm
    n = n_rows // tm

    def mix_specs(tile, **kw):
        row = lambda w: pl.BlockSpec((tm, w), lambda i: (tile(i), 0), **kw)
        prev8 = pl.BlockSpec((SUBLANES, CONV_WIDTH), lambda i: (jnp.maximum(tile(i) * n8 - 1, 0), 0), **kw)
        next8 = pl.BlockSpec((SUBLANES, CONV_WIDTH),
                             lambda i: (jnp.minimum((tile(i) + 1) * n8, t // SUBLANES - 1), 0), **kw)
        return [row(GLA_WIDTH), row(GLA_WIDTH), row(GLA_WIDTH), row(CONV_WIDTH), row(CONV_WIDTH),
                prev8, next8]

    streams = (o_f, o_b, sg, b, z, z, z)
    weights = (hn, cw, cn, wo, n2, wg, wu, wd, fn)
    return pl.pallas_call(
        functools.partial(_kernel_c, layout=layout, first_tile=t0, n_tiles=n),
        grid=(n,),
        in_specs=[pl.BlockSpec((tm, D_MODEL), lambda i: (i + t0, 0))]
        + mix_specs(lambda i: t0 + jnp.minimum(i + 1, n - 1))
        + mix_specs(lambda i: t0 + 0 * i, pipeline_mode=pl.Buffered(1))
        + [_resident(w.shape) for w in weights],
        out_specs=pl.BlockSpec((tm, D_MODEL), lambda i: (i, 0)),
        out_shape=jax.ShapeDtypeStruct((n_rows, D_MODEL), F32),
        scratch_shapes=[pltpu.VMEM((tm, D_FF), BF16)] + [pltpu.VMEM((tm, GLA_WIDTH + CONV_WIDTH), BF16)] * 2,
        compiler_params=pltpu.CompilerParams(
            dimension_semantics=("arbitrary",), vmem_limit_bytes=VMEM_LIMIT_BYTES),
        name="mix_out_ffn2",
    )(x1, *streams, *streams, *weights)


def kernel(x_prompt, x_sample, ffn1_norm, ffn1_w_gate, ffn1_w_up, ffn1_w_down, mix_norm, w_in,
           gate_fwd_w, gate_fwd_b, gate_bwd_w, gate_bwd_b, gla_head_norm, conv_w, conv_group_norm,
           w_out, ffn2_norm, ffn2_w_gate, ffn2_w_up, ffn2_w_down, final_norm):
    assert ffn1_norm.shape[0] == 1, "the final norm is fused after the (single) layer"
    l = 0
    o_r = 2 * GLA_KEY_WIDTH + 2 * GLA_WIDTH
    o_b = o_r + 2 * GATE_RANK
    zeros = jnp.zeros((GATE_RANK, GLA_KEY_WIDTH), F32)
    row = lambda a: a[l].reshape(1, -1)
    wgate = jnp.concatenate(
        [jnp.concatenate([gate_fwd_w[l], zeros], axis=1),
         jnp.concatenate([zeros, gate_bwd_w[l]], axis=1)], axis=0).astype(BF16)
    bgate = jnp.concatenate([gate_fwd_b[l], gate_bwd_b[l]]).reshape(1, -1)
    win = jnp.concatenate([w_in[l, :, :o_r], w_in[l, :, o_b:], w_in[l, :, o_r:o_b]], axis=1).astype(BF16)

    xp = x_prompt.reshape(-1, D_MODEL)
    xs = x_sample.reshape(-1, D_MODEL)
    layout = (xp.shape[0], x_prompt.shape[1], x_sample.shape[1])
    step_rows = GLA_STEP_TILES * GLA_TILE
    assert all(n % step_rows == 0 and n % TOKEN_TILE == 0 for n in layout)

    x1, q, k, v, sg, la, b, z = _call_a(
        xp, xs, row(ffn1_norm), ffn1_w_gate[l].astype(BF16), ffn1_w_up[l].astype(BF16),
        ffn1_w_down[l].astype(BF16), row(mix_norm), win, wgate, bgate)
    o_f, o_b_, wo, wg2, wu2, wd2 = _call_b(
        q, k, v, la, layout, (w_out[l], ffn2_w_gate[l], ffn2_w_up[l], ffn2_w_down[l]))
    c_args = (x1, o_f, o_b_, sg, b, z, row(gla_head_norm), conv_w[l], row(conv_group_norm),
              wo, row(ffn2_norm), wg2, wu2, wd2, final_norm.reshape(1, -1), layout)
    yp = _call_c(*c_args, first_row=0, n_rows=xp.shape[0])
    ys = _call_c(*c_args, first_row=xp.shape[0], n_rows=xs.shape[0])
    return yp.reshape(x_prompt.shape), ys.reshape(x_sample.shape)
```

```python
import functools

import jax
import jax.numpy as jnp
from jax import lax
from jax.experimental import pallas as pl
from jax.experimental.pallas import tpu as pltpu

F32 = jnp.float32
BF16 = jnp.bfloat16

D_MODEL = 1024
D_FF = 2816
GLA_WIDTH = 512
CONV_WIDTH = 512
GLA_HEADS = 4
GLA_DV = 128
GLA_KEY_WIDTH = 256
GLA_DK = 64
GATE_RANK = 16
GATE_TAU = 16.0
CHUNK = 64
CONV_GROUP_DIM = 64
FFN_RESIDUAL = 0.5
EPS = 1e-6
LANES = 128
SUBLANES = 8

TOKEN_TILE = 512
FF_CHUNK = 256
ROW_SPLIT = 2
GLA_TILE = 256
GLA_STEP_TILES = 4
VMEM_LIMIT_BYTES = 56 * 1024 * 1024

_NT = (((1,), (1,)), ((), ()))
_TN = (((0,), (0,)), ((), ()))


def _dot(a, b):
    return jnp.dot(a, b, preferred_element_type=F32)


def _rms(x, gain):
    inv = lax.rsqrt(jnp.mean(x * x, axis=-1, keepdims=True) + EPS)
    return (x * inv) * gain


def _row_parts(rows):
    return [slice(s * rows // ROW_SPLIT, (s + 1) * rows // ROW_SPLIT) for s in range(ROW_SPLIT)]


def _swiglu_hidden(xn_parts, wg_ref, wu_ref, h_ref, side_work=()):
    parts = _row_parts(h_ref.shape[0])
    xn = jnp.concatenate(xn_parts, axis=0)
    side_work = list(side_work)
    assert len(side_work) <= D_FF // FF_CHUNK
    for c in range(D_FF // FF_CHUNK):
        cols = slice(c * FF_CHUNK, (c + 1) * FF_CHUNK)
        for rs, lhs in (zip(parts, xn_parts) if c == 0 else [(slice(None), xn)]):
            g = _dot(lhs, wg_ref[:, cols])
            u = _dot(lhs, wu_ref[:, cols])
            h_ref[rs, cols] = ((g * jax.nn.sigmoid(g)) * u).astype(BF16)
        if c < len(side_work):
            side_work[c]()


def _resident(shape):
    return pl.BlockSpec(shape, lambda *_: (0,) * len(shape), pipeline_mode=pl.Buffered(1))


def _starts_sequence(block, rows, layout):
    prompt_rows, prompt_len, sample_len = layout
    n_prompt = prompt_rows // rows
    in_prompt = block < n_prompt
    return jnp.where(in_prompt, block % (prompt_len // rows) == 0,
                     (block - n_prompt) % (sample_len // rows) == 0)


def _ends_sequence(block, rows, layout):
    prompt_rows, prompt_len, sample_len = layout
    n_prompt = prompt_rows // rows
    in_prompt = block < n_prompt
    return jnp.where(in_prompt, (block + 1) % (prompt_len // rows) == 0,
                     (block - n_prompt + 1) % (sample_len // rows) == 0)


def _kernel_a(xp_ref, xs_ref, n1_ref, wg_ref, wu_ref, wd_ref, mn_ref, win_ref, wgate_ref, bgate_ref,
              x1_ref, qkla_ref, v_ref, sbz_ref, h_ref, *, n_prompt_tiles):
    n_qkvg = 2 * GLA_KEY_WIDTH + 2 * GLA_WIDTH
    n_bch = 3 * CONV_WIDTH
    x = jnp.where(pl.program_id(0) < n_prompt_tiles, xp_ref[...], xs_ref[...])
    halves = _row_parts(TOKEN_TILE)
    _swiglu_hidden([_rms(x[rs], n1_ref[...]).astype(BF16) for rs in halves], wg_ref, wu_ref, h_ref)
    ffn = [_dot(h_ref[rs, :], wd_ref[...]) for rs in halves]
    for rs, f in zip(halves, ffn):
        x1 = x[rs] + FFN_RESIDUAL * f
        x1_ref[rs, :] = x1
        u = _rms(x1, mn_ref[...]).astype(BF16)

        r = _dot(u, win_ref[:, n_qkvg + n_bch:])
        p = _dot(u, win_ref[:, :n_qkvg])
        t = _dot(r.astype(BF16), wgate_ref[...]) + bgate_ref[...]
        pb = _dot(u, win_ref[:, n_qkvg:n_qkvg + n_bch])

        log_sig = jnp.minimum(t, 0.0) - jnp.log(1.0 + jnp.exp(-jnp.abs(t)))
        qkla_ref[rs, 2 * GLA_KEY_WIDTH:] = log_sig * (1.0 / GATE_TAU)
        qkla_ref[rs, :GLA_KEY_WIDTH] = p[:, :GLA_KEY_WIDTH] * (GLA_DK ** -0.5)
        qkla_ref[rs, GLA_KEY_WIDTH:2 * GLA_KEY_WIDTH] = p[:, GLA_KEY_WIDTH:2 * GLA_KEY_WIDTH]
        v_ref[rs, :] = p[:, 2 * GLA_KEY_WIDTH:2 * GLA_KEY_WIDTH + GLA_WIDTH].astype(BF16)
        g = p[:, 2 * GLA_KEY_WIDTH + GLA_WIDTH:]
        sbz_ref[rs, :GLA_WIDTH] = g * jax.nn.sigmoid(g)
        sbz_ref[rs, GLA_WIDTH:GLA_WIDTH + CONV_WIDTH] = pb[:, :CONV_WIDTH]
        sbz_ref[rs, GLA_WIDTH + CONV_WIDTH:] = pb[:, CONV_WIDTH:2 * CONV_WIDTH] * pb[:, 2 * CONV_WIDTH:]


def _call_a(xp, xs, n1, wg, wu, wd, mn, win, wgate, bgate):
    tm = TOKEN_TILE
    n_p, n_s = xp.shape[0] // tm, xs.shape[0] // tm
    t = xp.shape[0] + xs.shape[0]
    row = lambda w: pl.BlockSpec((tm, w), lambda i: (i, 0))
    xp_spec = pl.BlockSpec((tm, D_MODEL), lambda i: (jnp.minimum(i, n_p - 1), 0))
    xs_spec = pl.BlockSpec((tm, D_MODEL), lambda i: (jnp.maximum(i - n_p, 0), 0))
    out_w = (D_MODEL, 4 * GLA_KEY_WIDTH, GLA_WIDTH, GLA_WIDTH + 2 * CONV_WIDTH)
    out_dtype = (F32, F32, BF16, F32)
    weights = (n1, wg, wu, wd, mn, win, wgate, bgate)
    return pl.pallas_call(
        functools.partial(_kernel_a, n_prompt_tiles=n_p),
        grid=(n_p + n_s,),
        in_specs=[xp_spec, xs_spec] + [_resident(w.shape) for w in weights],
        out_specs=[row(w) for w in out_w],
        out_shape=[jax.ShapeDtypeStruct((t, w), dt) for w, dt in zip(out_w, out_dtype)],
        scratch_shapes=[pltpu.VMEM((tm, D_FF), BF16)],
        compiler_params=pltpu.CompilerParams(
            dimension_semantics=("arbitrary",), vmem_limit_bytes=VMEM_LIMIT_BYTES),
        name="ffn1_inproj",
    )(xp, xs, *weights)


HEADS_PER_PAIR = LANES // GLA_DK
N_PAIRS = GLA_HEADS // HEADS_PER_PAIR
PAIR_DK = HEADS_PER_PAIR * GLA_DK
PAIR_DV = HEADS_PER_PAIR * GLA_DV


def _gla_decays(jobs):
    rows = GLA_TILE
    n_chunks = rows // CHUNK
    ri = lax.broadcasted_iota(jnp.int32, (rows, rows), 0)
    ci = lax.broadcasted_iota(jnp.int32, (rows, rows), 1)
    same_chunk = (ri // CHUNK) == (ci // CHUNK)
    ctxs = []
    for (q_ref, k_ref, v_ref, la_ref, o_ref, off, reverse) in jobs:
        rs = pl.ds(off, rows)
        keep = same_chunk & ((ci >= ri) if reverse else (ci <= ri))
        tri = jnp.where(keep, 1.0, 0.0).astype(BF16)
        la = la_ref[rs, :]
        la_hi = la.astype(BF16)
        la_lo = (la - la_hi.astype(F32)).astype(BF16)
        cum = _dot(tri, la_hi) + _dot(tri, la_lo)
        totals = [cum[c * CHUNK:c * CHUNK + 1] if reverse else cum[(c + 1) * CHUNK - 1:(c + 1) * CHUNK]
                  for c in range(n_chunks)]
        ctxs.append(dict(keep=keep, cum=cum, totals=totals, o_ref=o_ref, off=off, reverse=reverse))
    for ctx, (q_ref, k_ref, v_ref, _, _, off, _) in zip(ctxs, jobs):
        rs = pl.ds(off, rows)
        cum = ctx.pop("cum")
        totals = ctx.pop("totals")
        total_rows = jnp.concatenate(
            [jnp.broadcast_to(t, (CHUNK, GLA_KEY_WIDTH)) for t in totals], axis=0)
        padded = jnp.concatenate(totals + [jnp.zeros((LANES - n_chunks, GLA_KEY_WIDTH), F32)], axis=0)
        ctx["decay_cols"] = jnp.exp(padded.T)
        k = k_ref[rs, :]
        ctx["qd"] = q_ref[rs, :] * jnp.exp(cum)
        ctx["qd_bf"] = ctx["qd"].astype(BF16)
        ctx["k_inv"] = (k * jnp.exp(-cum)).astype(BF16)
        ctx["k_tail"] = (k * jnp.exp(total_rows - cum)).astype(BF16)
        ctx["v"] = v_ref[rs, :]
    return ctxs


def _gla_products(ctxs):
    rows = GLA_TILE
    n_chunks = rows // CHUNK
    lane = lax.broadcasted_iota(jnp.int32, (1, GLA_KEY_WIDTH), 1)
    same_head = (lax.broadcasted_iota(jnp.int32, (PAIR_DK, PAIR_DV), 0) // GLA_DK
                 == lax.broadcasted_iota(jnp.int32, (PAIR_DK, PAIR_DV), 1) // GLA_DV)
    for ctx in ctxs:
        q_heads = jnp.concatenate(
            [jnp.where((lane // GLA_DK) == h, ctx["qd"], 0.0) for h in range(GLA_HEADS)], axis=0)
        ctx["s"] = lax.dot_general(q_heads.astype(BF16), ctx["k_inv"], _NT,
                                   preferred_element_type=F32)
    for ctx in ctxs:
        ctx["intra"] = []
    for h in range(GLA_HEADS):
        for ctx in ctxs:
            s = jnp.where(ctx["keep"], ctx["s"][h * rows:(h + 1) * rows], 0.0).astype(BF16)
            ctx["intra"].append(_dot(s, ctx["v"][:, h * GLA_DV:(h + 1) * GLA_DV]))
    for ctx in ctxs:
        ctx["incr"] = []
    for c in range(n_chunks):
        rs = slice(c * CHUNK, (c + 1) * CHUNK)
        for ctx in ctxs:
            ctx["incr"].append([
                jnp.where(same_head,
                          lax.dot_general(ctx["k_tail"][rs, p * PAIR_DK:(p + 1) * PAIR_DK],
                                          ctx["v"][rs, p * PAIR_DV:(p + 1) * PAIR_DV], _TN,
                                          preferred_element_type=F32), 0.0)
                for p in range(N_PAIRS)])
    for ctx in ctxs:
        del ctx["qd"], ctx["k_inv"], ctx["k_tail"], ctx["v"], ctx["keep"], ctx["s"]


def _gla_recurrence(ctxs, states):
    n_chunks = GLA_TILE // CHUNK
    states = list(states)
    for step in range(n_chunks):
        for i, ctx in enumerate(ctxs):
            c = n_chunks - 1 - step if ctx["reverse"] else step
            rs = slice(c * CHUNK, (c + 1) * CHUNK)
            st = states[i]
            inter = jnp.concatenate(
                [_dot(ctx["qd_bf"][rs, p * PAIR_DK:(p + 1) * PAIR_DK], st[p].astype(BF16))
                 for p in range(N_PAIRS)], axis=1)
            ctx["o_ref"][pl.ds(ctx["off"] + c * CHUNK, CHUNK), :] = (
                jnp.concatenate([x[rs] for x in ctx["intra"]], axis=1) + inter)
            states[i] = [st[p] * ctx["decay_cols"][p * PAIR_DK:(p + 1) * PAIR_DK, c:c + 1]
                         + ctx["incr"][c][p] for p in range(N_PAIRS)]
    return states


def _kernel_b(*refs, layout, n_cast):
    (qf_ref, kf_ref, vf_ref, laf_ref, qb_ref, kb_ref, vb_ref, lab_ref), refs = refs[:8], refs[8:]
    cast_in, (of_ref, ob_ref), refs = refs[:n_cast], refs[n_cast:n_cast + 2], refs[n_cast + 2:]
    cast_out, (stf_ref, stb_ref) = refs[:n_cast], refs[n_cast:]
    for src, dst in zip(cast_in, cast_out):
        dst[...] = src[...].astype(BF16)
    j = pl.program_id(0)
    n_steps = pl.num_programs(0)
    step_rows = GLA_STEP_TILES * GLA_TILE

    @pl.when(_starts_sequence(j, step_rows, layout))
    def _():
        stf_ref[...] = jnp.zeros_like(stf_ref)

    @pl.when(_ends_sequence(n_steps - 1 - j, step_rows, layout))
    def _():
        stb_ref[...] = jnp.zeros_like(stb_ref)

    def jobs(t):
        return [(qf_ref, kf_ref, vf_ref, laf_ref, of_ref, t * GLA_TILE, False),
                (qb_ref, kb_ref, vb_ref, lab_ref, ob_ref, (GLA_STEP_TILES - 1 - t) * GLA_TILE, True)]

    states = [[ref[p * PAIR_DK:(p + 1) * PAIR_DK, :] for p in range(N_PAIRS)]
              for ref in (stf_ref, stb_ref)]
    ctxs = [None] * GLA_STEP_TILES
    ctxs[0] = _gla_decays(jobs(0))
    for t in range(GLA_STEP_TILES):
        if t + 1 < GLA_STEP_TILES:
            ctxs[t + 1] = _gla_decays(jobs(t + 1))
        _gla_products(ctxs[t])
        if t >= 1:
            states = _gla_recurrence(ctxs[t - 1], states)
    states = _gla_recurrence(ctxs[GLA_STEP_TILES - 1], states)
    for ref, st in zip((stf_ref, stb_ref), states):
        for p in range(N_PAIRS):
            ref[p * PAIR_DK:(p + 1) * PAIR_DK, :] = st[p]


def _cast_slab_spec(shape, n_steps):
    bf16_rows = 2 * SUBLANES
    d = max(d for d in range(1, n_steps + 1)
            if n_steps % d == 0 and shape[0] % d == 0 and (shape[0] // d) % bf16_rows == 0)
    return pl.BlockSpec((shape[0] // d, shape[1]), lambda j: (jnp.minimum(j, d - 1), 0))


def _call_b(qkla, v, layout, cast_weights):
    t = v.shape[0]
    rows = GLA_STEP_TILES * GLA_TILE
    n = t // rows
    fwd = lambda j: (j, 0)
    bwd = lambda j: (n - 1 - j, 0)
    kw = GLA_KEY_WIDTH
    spec = lambda w, im: pl.BlockSpec((rows, w), im)
    col = lambda c, reverse: pl.BlockSpec((rows, kw), lambda j: (n - 1 - j if reverse else j, c))
    cast_specs = [_cast_slab_spec(w.shape, n) for w in cast_weights]
    return pl.pallas_call(
        functools.partial(_kernel_b, layout=layout, n_cast=len(cast_weights)),
        grid=(n,),
        in_specs=[col(0, False), col(1, False), spec(GLA_WIDTH, fwd), col(2, False),
                  col(0, True), col(1, True), spec(GLA_WIDTH, bwd), col(3, True)] + cast_specs,
        out_specs=[spec(GLA_WIDTH, fwd), spec(GLA_WIDTH, bwd)] + cast_specs,
        out_shape=[jax.ShapeDtypeStruct((t, GLA_WIDTH), F32)] * 2
        + [jax.ShapeDtypeStruct(w.shape, BF16) for w in cast_weights],
        scratch_shapes=[pltpu.VMEM((N_PAIRS * PAIR_DK, PAIR_DV), F32)] * 2,
        compiler_params=pltpu.CompilerParams(
            dimension_semantics=("arbitrary",), vmem_limit_bytes=VMEM_LIMIT_BYTES),
        name="gla_bidir",
    )(qkla, qkla, v, qkla, qkla, qkla, v, qkla, *cast_weights)


def _group_inv_rms(y):
    groups_per_block = LANES // CONV_GROUP_DIM
    lane_group = lax.broadcasted_iota(jnp.int32, (1, LANES), 1) // CONV_GROUP_DIM
    out = []
    for j in range(CONV_WIDTH // LANES):
        blk = y[:, j * LANES:(j + 1) * LANES]
        sq = blk * blk
        total = None
        for g in range(groups_per_block):
            total_g = jnp.sum(jnp.where(lane_group == g, sq, 0.0), axis=-1, keepdims=True)
            total = total_g if total is None else jnp.where(lane_group == g, total_g, total)
        out.append(jnp.broadcast_to(lax.rsqrt(total * (1.0 / CONV_GROUP_DIM) + EPS), blk.shape))
    return jnp.concatenate(out, axis=1)


def _mixer_input(r0, r1, tile, layout, of_ref, ob_ref, sbz_ref, zp_ref, zn_ref, hn_ref, cw_ref, cn_ref):
    tm = TOKEN_TILE
    n = r1 - r0
    sg_cols = slice(0, GLA_WIDTH)
    b_cols = slice(GLA_WIDTH, GLA_WIDTH + CONV_WIDTH)
    z_cols = slice(GLA_WIDTH + CONV_WIDTH, GLA_WIDTH + 2 * CONV_WIDTH)
    o = of_ref[r0:r1, :] + ob_ref[r0:r1, :]
    heads = []
    for h in range(GLA_HEADS):
        oh = o[:, h * GLA_DV:(h + 1) * GLA_DV]
        heads.append(oh * lax.rsqrt(jnp.mean(oh * oh, axis=-1, keepdims=True) + EPS))
    gla_out = (jnp.concatenate(heads, axis=1) * hn_ref[...]) * sbz_ref[r0:r1, sg_cols]

    above = (jnp.where(_starts_sequence(tile, tm, layout), 0.0, zp_ref[...]) if r0 == 0
             else sbz_ref[r0 - SUBLANES:r0, z_cols])
    below = (jnp.where(_ends_sequence(tile, tm, layout), 0.0, zn_ref[...]) if r1 == tm
             else sbz_ref[r1:r1 + SUBLANES, z_cols])
    slab = jnp.concatenate([above, sbz_ref[r0:r1, z_cols], below], axis=0)
    inner = slice(SUBLANES, SUBLANES + n)
    z_prev = pltpu.roll(slab, 1, axis=0)[inner]
    z_next = pltpu.roll(slab, n + 2 * SUBLANES - 1, axis=0)[inner]
    conv = (z_prev * cw_ref[0:1, :] + slab[inner] * cw_ref[1:2, :]) + z_next * cw_ref[2:3, :]
    yc = sbz_ref[r0:r1, b_cols] * conv
    conv_out = (yc * _group_inv_rms(yc)) * cn_ref[...]
    return jnp.concatenate([gla_out, conv_out], axis=1).astype(BF16)


N_MIX_STREAMS = 5
MIX_ROW_CHUNK = 64


def _kernel_c(x1_ref, *refs, layout, first_tile, n_tiles):
    next_refs, first_refs, refs = (refs[:N_MIX_STREAMS], refs[N_MIX_STREAMS:2 * N_MIX_STREAMS],
                                   refs[2 * N_MIX_STREAMS:])
    (hn_ref, cw_ref, cn_ref, wo_ref, n2_ref, wg_ref, wu_ref, wd_ref, fn_ref,
     y_ref, h_ref, mix_ref, mix_next_ref) = refs
    i = pl.program_id(0)
    tm = TOKEN_TILE
    row_chunks = [(r, r + MIX_ROW_CHUNK) for r in range(0, tm, MIX_ROW_CHUNK)]

    @pl.when(i == 0)
    def _():
        for r0, r1 in row_chunks:
            mix_ref[r0:r1, :] = _mixer_input(r0, r1, first_tile, layout, *first_refs, hn_ref, cw_ref, cn_ref)

    next_tile = first_tile + jnp.minimum(i + 1, n_tiles - 1)

    def next_chunk(r0, r1):
        mix_next_ref[r0:r1, :] = _mixer_input(r0, r1, next_tile, layout, *next_refs, hn_ref, cw_ref, cn_ref)

    halves = _row_parts(tm)
    x2 = [x1_ref[rs, :] + _dot(mix_ref[rs, :], wo_ref[...]) for rs in halves]
    _swiglu_hidden([_rms(x, n2_ref[...]).astype(BF16) for x in x2], wg_ref, wu_ref, h_ref,
                   side_work=[functools.partial(next_chunk, r0, r1) for r0, r1 in row_chunks])
    ffn = [_dot(h_ref[rs, :], wd_ref[...]) for rs in halves]
    for rs, x, f in zip(halves, x2, ffn):
        y_ref[rs, :] = _rms(x + FFN_RESIDUAL * f, fn_ref[...])
    mix_ref[...] = mix_next_ref[...]


def _call_c(x1, o_f, o_b, sbz, hn, cw, cn, wo, n2, wg, wu, wd, fn, layout, first_row, n_rows):
    t = x1.shape[0]
    tm = TOKEN_TILE
    n8 = tm // SUBLANES
    t0 = first_row // tm
    n = n_rows // tm

    def mix_specs(tile, **kw):
        row = lambda w: pl.BlockSpec((tm, w), lambda i: (tile(i), 0), **kw)
        z_col = (GLA_WIDTH + CONV_WIDTH) // CONV_WIDTH
        prev8 = pl.BlockSpec((SUBLANES, CONV_WIDTH),
                             lambda i: (jnp.maximum(tile(i) * n8 - 1, 0), z_col), **kw)
        next8 = pl.BlockSpec((SUBLANES, CONV_WIDTH),
                             lambda i: (jnp.minimum((tile(i) + 1) * n8, t // SUBLANES - 1), z_col), **kw)
        return [row(GLA_WIDTH), row(GLA_WIDTH), row(GLA_WIDTH + 2 * CONV_WIDTH), prev8, next8]

    streams = (o_f, o_b, sbz, sbz, sbz)
    weights = (hn, cw, cn, wo, n2, wg, wu, wd, fn)
    return pl.pallas_call(
        functools.partial(_kernel_c, layout=layout, first_tile=t0, n_tiles=n),
        grid=(n,),
        in_specs=[pl.BlockSpec((tm, D_MODEL), lambda i: (i + t0, 0))]
        + mix_specs(lambda i: t0 + jnp.minimum(i + 1, n - 1))
        + mix_specs(lambda i: t0 + 0 * i, pipeline_mode=pl.Buffered(1))
        + [_resident(w.shape) for w in weights],
        out_specs=pl.BlockSpec((tm, D_MODEL), lambda i: (i, 0)),
        out_shape=jax.ShapeDtypeStruct((n_rows, D_MODEL), F32),
        scratch_shapes=[pltpu.VMEM((tm, D_FF), BF16)] + [pltpu.VMEM((tm, GLA_WIDTH + CONV_WIDTH), BF16)] * 2,
        compiler_params=pltpu.CompilerParams(
            dimension_semantics=("arbitrary",), vmem_limit_bytes=VMEM_LIMIT_BYTES),
        name="mix_out_ffn2",
    )(x1, *streams, *streams, *weights)


def kernel(x_prompt, x_sample, ffn1_norm, ffn1_w_gate, ffn1_w_up, ffn1_w_down, mix_norm, w_in,
           gate_fwd_w, gate_fwd_b, gate_bwd_w, gate_bwd_b, gla_head_norm, conv_w, conv_group_norm,
           w_out, ffn2_norm, ffn2_w_gate, ffn2_w_up, ffn2_w_down, final_norm):
    assert ffn1_norm.shape[0] == 1, "the final norm is fused after the (single) layer"
    l = 0
    o_r = 2 * GLA_KEY_WIDTH + 2 * GLA_WIDTH
    o_b = o_r + 2 * GATE_RANK
    zeros = jnp.zeros((GATE_RANK, GLA_KEY_WIDTH), F32)
    row = lambda a: a[l].reshape(1, -1)
    wgate = jnp.concatenate(
        [jnp.concatenate([gate_fwd_w[l], zeros], axis=1),
         jnp.concatenate([zeros, gate_bwd_w[l]], axis=1)], axis=0).astype(BF16)
    bgate = jnp.concatenate([gate_fwd_b[l], gate_bwd_b[l]]).reshape(1, -1)
    win = jnp.concatenate([w_in[l, :, :o_r], w_in[l, :, o_b:], w_in[l, :, o_r:o_b]], axis=1).astype(BF16)

    xp = x_prompt.reshape(-1, D_MODEL)
    xs = x_sample.reshape(-1, D_MODEL)
    layout = (xp.shape[0], x_prompt.shape[1], x_sample.shape[1])
    step_rows = GLA_STEP_TILES * GLA_TILE
    assert all(n % step_rows == 0 and n % TOKEN_TILE == 0 for n in layout)

    x1, qkla, v, sbz = _call_a(
        xp, xs, row(ffn1_norm), ffn1_w_gate[l].astype(BF16), ffn1_w_up[l].astype(BF16),
        ffn1_w_down[l].astype(BF16), row(mix_norm), win, wgate, bgate)
    o_f, o_b_, wo, wg2, wu2, wd2 = _call_b(
        qkla, v, layout, (w_out[l], ffn2_w_gate[l], ffn2_w_up[l], ffn2_w_down[l]))
    c_args = (x1, o_f, o_b_, sbz, row(gla_head_norm), conv_w[l], row(conv_group_norm),
              wo, row(ffn2_norm), wg2, wu2, wd2, final_norm.reshape(1, -1), layout)
    yp = _call_c(*c_args, first_row=0, n_rows=xp.shape[0])
    ys = _call_c(*c_args, first_row=xp.shape[0], n_rows=xs.shape[0])
    return yp.reshape(x_prompt.shape), ys.reshape(x_sample.shape)
```
